```python
import jax, jax.numpy as jnp
from jax import lax
import numpy as np

D_MODEL = 1024
BATCH = 8
SEQ = 4096
DEPTH = 2

N_MEM = 256
D_MIX = 2 * D_MODEL
MLA_HEADS = 16
MLA_NOPE = 64
MLA_ROPE = 32
MLA_V = 64
MLA_Q_RANK = 384
MLA_KV_RANK = 256
ROPE_BASE = 10000.0
Q_BLOCK = 128
SWA_HEADS = 8
SWA_KV_HEADS = 2
SWA_HEAD_DIM = 64
WINDOW = 128
MEM_HEADS = 4
MEM_HEAD_DIM = 128
EPS = 1e-6
NEG = -1e30

MLA_WIDTH = MLA_HEADS * MLA_V
SWA_WIDTH = SWA_HEADS * SWA_HEAD_DIM
SWA_KV_WIDTH = SWA_KV_HEADS * SWA_HEAD_DIM
MEM_WIDTH = MEM_HEADS * MEM_HEAD_DIM
SPLITS = (MLA_Q_RANK, MLA_KV_RANK, MLA_ROPE, MLA_WIDTH,
          SWA_WIDTH, SWA_KV_WIDTH, SWA_KV_WIDTH, SWA_WIDTH,
          MEM_WIDTH, MEM_WIDTH)
D_IN_PROJ = sum(SPLITS)
SPLIT_IDX = tuple(int(i) for i in np.cumsum(SPLITS)[:-1])

kernel_name = "hymba_mla_swa_sink_alibi_memxattn"


def rmsnorm(x, g):
    xf = x.astype(jnp.float32)
    y = xf * lax.rsqrt(jnp.mean(xf * xf, axis=-1, keepdims=True) + EPS)
    return (y * g.astype(jnp.float32)).astype(x.dtype)


def rope_tables(seq, dim):
    inv = ROPE_BASE ** (-jnp.arange(0, dim, 2, dtype=jnp.float32) / dim)
    ang = jnp.arange(seq, dtype=jnp.float32)[:, None] * inv[None, :]
    return jnp.cos(ang), jnp.sin(ang)


def apply_rope(x, cos, sin):
    x1, x2 = jnp.split(x, 2, axis=-1)
    c = cos[None, :, None, :].astype(x.dtype)
    s = sin[None, :, None, :].astype(x.dtype)
    return jnp.concatenate([x1 * c - x2 * s, x1 * s + x2 * c], axis=-1)


def alibi_slopes(n):
    return jnp.exp2(-8.0 * jnp.arange(1, n + 1, dtype=jnp.float32) / n)


def mla_group(c_q, c_kv, k_rope_in, q_norm, w_uq, kv_norm, w_ukv):
    B, S, _ = c_q.shape
    cos, sin = rope_tables(S, MLA_ROPE)
    q = (rmsnorm(c_q, q_norm) @ w_uq).reshape(B, S, MLA_HEADS, MLA_NOPE + MLA_ROPE)
    q_nope, q_pe = q[..., :MLA_NOPE], apply_rope(q[..., MLA_NOPE:], cos, sin)
    kv = (rmsnorm(c_kv, kv_norm) @ w_ukv).reshape(B, S, MLA_HEADS, MLA_NOPE + MLA_V)
    k_nope, v = kv[..., :MLA_NOPE], kv[..., MLA_NOPE:]
    k_pe = apply_rope(k_rope_in[:, :, None, :], cos, sin)[:, :, 0, :]
    scale = (MLA_NOPE + MLA_ROPE) ** -0.5
    nblk = S // Q_BLOCK
    qn_b = q_nope.reshape(B, nblk, Q_BLOCK, MLA_HEADS, MLA_NOPE).transpose(1, 0, 2, 3, 4)
    qp_b = q_pe.reshape(B, nblk, Q_BLOCK, MLA_HEADS, MLA_ROPE).transpose(1, 0, 2, 3, 4)
    kpos = jnp.arange(S)

    def block(args):
        qn, qp, i = args
        s = (jnp.einsum('bqhd,bkhd->bhqk', qn, k_nope)
             + jnp.einsum('bqhr,bkr->bhqk', qp, k_pe)).astype(jnp.float32) * scale
        qpos = i * Q_BLOCK + jnp.arange(Q_BLOCK)
        causal = kpos[None, :] <= qpos[:, None]
        s = jnp.where(causal[None, None], s, NEG)
        p = jax.nn.softmax(s, axis=-1).astype(v.dtype)
        return jnp.einsum('bhqk,bkhd->bqhd', p, v)

    out = lax.map(block, (qn_b, qp_b, jnp.arange(nblk)))
    return out.transpose(1, 0, 2, 3, 4).reshape(B, S, MLA_WIDTH)


def swa_group(q, k, v, sinks):
    B, S, _ = q.shape
    W = WINDOW
    G = SWA_HEADS // SWA_KV_HEADS
    nblk = S // W
    qb = q.reshape(B, nblk, W, SWA_KV_HEADS, G, SWA_HEAD_DIM)
    k = k.reshape(B, S, SWA_KV_HEADS, SWA_HEAD_DIM)
    v = v.reshape(B, S, SWA_KV_HEADS, SWA_HEAD_DIM)
    pad = ((0, 0), (W, 0), (0, 0), (0, 0))
    kp, vp = jnp.pad(k, pad), jnp.pad(v, pad)
    shp = (B, nblk, W, SWA_KV_HEADS, SWA_HEAD_DIM)
    kb = jnp.concatenate([kp[:, :S].reshape(shp), k.reshape(shp)], axis=2)
    vb = jnp.concatenate([vp[:, :S].reshape(shp), v.reshape(shp)], axis=2)
    s = jnp.einsum('bnqkgd,bnskd->bnkgqs', qb, kb).astype(jnp.float32) * (SWA_HEAD_DIM ** -0.5)
    i = jnp.arange(W)[:, None]
    j = jnp.arange(2 * W)[None, :]
    dist = W + i - j
    kglob = jnp.arange(nblk)[:, None, None] * W + j[None] - W
    mask = (dist >= 0)[None] & (dist < WINDOW)[None] & (kglob >= 0)
    slopes = alibi_slopes(SWA_HEADS).reshape(SWA_KV_HEADS, G)
    s = s - slopes[None, None, :, :, None, None] * dist.astype(jnp.float32)[None, None, None, None]
    s = jnp.where(mask[None, :, None, None], s, NEG)
    sink = sinks.astype(jnp.float32).reshape(SWA_KV_HEADS, G)[None, None, :, :, None, None]
    m = jnp.maximum(jnp.max(s, axis=-1, keepdims=True), sink)
    e = jnp.exp(s - m)
    p = (e / (jnp.sum(e, axis=-1, keepdims=True) + jnp.exp(sink - m))).astype(v.dtype)
    out = jnp.einsum('bnkgqs,bnskd->bnqkgd', p, vb)
    return out.reshape(B, S, SWA_WIDTH)


def mem_group(q, mem_n, w_mem_kv):
    B, S, _ = q.shape
    M = mem_n.shape[1]
    kv = mem_n @ w_mem_kv
    k = kv[..., :MEM_WIDTH].reshape(B, M, MEM_HEADS, MEM_HEAD_DIM)
    v = kv[..., MEM_WIDTH:].reshape(B, M, MEM_HEADS, MEM_HEAD_DIM)
    q = q.reshape(B, S, MEM_HEADS, MEM_HEAD_DIM)
    s = jnp.einsum('bshd,bmhd->bhsm', q, k).astype(jnp.float32) * (MEM_HEAD_DIM ** -0.5)
    p = jax.nn.softmax(s, axis=-1).astype(v.dtype)
    return jnp.einsum('bhsm,bmhd->bshd', p, v).reshape(B, S, MEM_WIDTH)


def setup_inputs(seed: int = 0) -> dict:
    key = jax.random.key(seed)
    ks = jax.random.split(key, 16)
    f32 = jnp.float32

    def nrm(k, shape, fan_in):
        return jax.random.normal(k, shape, f32) * (fan_in ** -0.5)

    def gain(k, shape):
        return 1.0 + 0.02 * jax.random.normal(k, shape, f32)

    return {
        "x": jax.random.normal(ks[0], (BATCH, SEQ, D_MODEL), f32),
        "mem": jax.random.normal(ks[1], (BATCH, N_MEM, D_MODEL), f32),
        "attn_norm": gain(ks[2], (DEPTH, D_MODEL)),
        "w_in": nrm(ks[3], (DEPTH, D_MODEL, D_IN_PROJ), D_MODEL),
        "mla_q_norm": gain(ks[4], (DEPTH, MLA_Q_RANK)),
        "w_uq": nrm(ks[5], (DEPTH, MLA_Q_RANK, MLA_HEADS * (MLA_NOPE + MLA_ROPE)), MLA_Q_RANK),
        "mla_kv_norm": gain(ks[6], (DEPTH, MLA_KV_RANK)),
        "w_ukv": nrm(ks[7], (DEPTH, MLA_KV_RANK, MLA_HEADS * (MLA_NOPE + MLA_V)), MLA_KV_RANK),
        "swa_sinks": jax.random.normal(ks[8], (DEPTH, SWA_HEADS), f32),
        "mem_norm": gain(ks[9], (DEPTH, D_MODEL)),
        "w_mem_kv": nrm(ks[10], (DEPTH, D_MODEL, 2 * MEM_WIDTH), D_MODEL),
        "w_out": nrm(ks[11], (DEPTH, D_MIX, D_MODEL), D_MIX),
        "final_norm": gain(ks[12], (D_MODEL,)),
    }


def reference(x, mem, attn_norm, w_in, mla_q_norm, w_uq, mla_kv_norm, w_ukv,
              swa_sinks, mem_norm, w_mem_kv, w_out, final_norm):
    for l in range(DEPTH):
        h = rmsnorm(x, attn_norm[l])
        proj = h @ w_in[l]
        (c_q, c_kv, k_rope, z_mla, q_swa, k_swa, v_swa, z_swa,
         q_mem, z_mem) = jnp.split(proj, SPLIT_IDX, axis=-1)
        y_mla = mla_group(c_q, c_kv, k_rope, mla_q_norm[l], w_uq[l],
                          mla_kv_norm[l], w_ukv[l]) * jax.nn.silu(z_mla)
        y_swa = swa_group(q_swa, k_swa, v_swa, swa_sinks[l]) * jax.nn.silu(z_swa)
        y_mem = mem_group(q_mem, rmsnorm(mem, mem_norm[l]), w_mem_kv[l]) * jax.nn.silu(z_mem)
        y = jnp.concatenate([y_mla, y_swa, y_mem], axis=-1) @ w_out[l]
        x = x + y
    return rmsnorm(x, final_norm)
```

```python
import functools

import jax
import jax.numpy as jnp
import numpy as np
from jax import lax
from jax.experimental import pallas as pl
from jax.experimental.pallas import tpu as pltpu

F32 = jnp.float32
BF16 = jnp.bfloat16

D_MODEL = 1024
MLA_HEADS = 16
MLA_NOPE = 64
MLA_ROPE = 32
MLA_V = 64
MLA_Q_RANK = 384
MLA_KV_RANK = 256
ROPE_BASE = 10000.0
SWA_HEADS = 8
SWA_KV_HEADS = 2
SWA_HEAD_DIM = 64
SWA_GROUP = SWA_HEADS // SWA_KV_HEADS
WINDOW = 128
MEM_HEADS = 4
MEM_HEAD_DIM = 128
EPS = 1e-6
NEG = -1e30

MLA_WIDTH = MLA_HEADS * MLA_V
SWA_WIDTH = SWA_HEADS * SWA_HEAD_DIM
SWA_KV_WIDTH = SWA_KV_HEADS * SWA_HEAD_DIM
MEM_WIDTH = MEM_HEADS * MEM_HEAD_DIM
SPLITS = (MLA_Q_RANK, MLA_KV_RANK, MLA_ROPE, MLA_WIDTH, SWA_WIDTH, SWA_KV_WIDTH, SWA_KV_WIDTH,
          SWA_WIDTH, MEM_WIDTH, MEM_WIDTH)
SPLIT_IDX = tuple(int(i) for i in np.cumsum(SPLITS)[:-1])

LANES = 128
MLA_HEAD_PAD = 128
HALF_ROPE = MLA_ROPE // 2
ATT_TILE = 256
PROJ_TILE = 256
SIDE_TILE = 256
OUT_TILE = 512
VMEM_LIMIT = 48 * 1024 * 1024

C_Q = 0
C_KV = C_Q + MLA_Q_RANK
C_KR = C_KV + MLA_KV_RANK
C_ZMLA = C_KR + LANES
C_QSWA = C_ZMLA + MLA_WIDTH
C_KSWA = C_QSWA + SWA_WIDTH
C_VSWA = C_KSWA + 2 * SWA_KV_WIDTH
C_ZSWA = C_VSWA + 2 * SWA_KV_WIDTH
C_QMEM = C_ZSWA + SWA_WIDTH
C_ZMEM = C_QMEM + MEM_WIDTH
W_IN_COLS = C_ZMEM + MEM_WIDTH

S_QSWA = 0
S_KSWA = S_QSWA + SWA_WIDTH
S_VSWA = S_KSWA + 2 * SWA_KV_WIDTH
S_GSWA = S_VSWA + 2 * SWA_KV_WIDTH
S_QMEM = S_GSWA + SWA_WIDTH
S_GMEM = S_QMEM + MEM_WIDTH
SIDE_COLS = S_GMEM + MEM_WIDTH

NT_DIMS = (((1,), (1,)), ((), ()))


def _rms(x, g):
    return x * lax.rsqrt(jnp.mean(x * x, axis=-1, keepdims=True) + EPS) * g


def _silu(z):
    return z * jax.nn.sigmoid(z)


def _const_spec(shape):
    nd = len(shape)
    return pl.BlockSpec(shape, lambda *_: (0,) * nd, pipeline_mode=pl.Buffered(1))


def _proj_kernel(x_ref, an_ref, win_ref, qn_ref, wuqt_ref, kvn_ref, wuk_ref, eab_ref, wuvt_ref,
                 cost_ref, sint_ref, cc_ref, ss_ref,
                 qt_ref, k_ref, vt_ref, gmla_ref, side_ref):
    hn = _rms(x_ref[0], an_ref[...]).astype(BF16)

    def proj(a, b):
        return jnp.dot(hn, win_ref[:, a:b], preferred_element_type=F32)

    cqn = _rms(proj(C_Q, C_KV), qn_ref[...]).astype(BF16)
    scale = (MLA_NOPE + MLA_ROPE) ** -0.5
    qt = lax.dot_general(wuqt_ref[...], cqn, NT_DIMS, preferred_element_type=F32) * scale
    c = cost_ref[...]
    s = sint_ref[...]
    zero_rows = jnp.zeros((MLA_HEAD_PAD - MLA_NOPE - MLA_ROPE, qt.shape[1]), BF16)
    for h in range(MLA_HEADS):
        r = h * MLA_HEAD_PAD
        x1 = qt[r + MLA_NOPE:r + MLA_NOPE + HALF_ROPE]
        x2 = qt[r + MLA_NOPE + HALF_ROPE:r + MLA_NOPE + MLA_ROPE]
        qt_ref[0, h, 0, 0:MLA_NOPE, :] = qt[r:r + MLA_NOPE].astype(BF16)
        qt_ref[0, h, 0, MLA_NOPE:MLA_NOPE + HALF_ROPE, :] = (x1 * c - x2 * s).astype(BF16)
        qt_ref[0, h, 0, MLA_NOPE + HALF_ROPE:MLA_NOPE + MLA_ROPE, :] = (x1 * s + x2 * c).astype(BF16)
        qt_ref[0, h, 0, MLA_NOPE + MLA_ROPE:MLA_HEAD_PAD, :] = zero_rows

    ckvn = _rms(proj(C_KV, C_KR), kvn_ref[...]).astype(BF16)
    kr = proj(C_KR, C_ZMLA)
    krab = jnp.concatenate([kr * cc_ref[...], kr * ss_ref[...]], axis=1).astype(BF16)
    kk = (jnp.dot(ckvn, wuk_ref[...], preferred_element_type=F32)
          + jnp.dot(krab, eab_ref[...], preferred_element_type=F32))
    for h in range(MLA_HEADS):
        k_ref[0, h, :, :] = kk[:, h * MLA_HEAD_PAD:(h + 1) * MLA_HEAD_PAD].astype(BF16)

    vt = lax.dot_general(wuvt_ref[...], ckvn, NT_DIMS, preferred_element_type=F32)
    for h in range(MLA_HEADS):
        vt_ref[0, h, 0, :, :] = vt[h * MLA_V:(h + 1) * MLA_V].astype(BF16)

    gmla_ref[0] = _silu(proj(C_ZMLA, C_QSWA)).astype(BF16)
    side_ref[0, :, S_QSWA:S_KSWA] = (proj(C_QSWA, C_KSWA) * (SWA_HEAD_DIM ** -0.5)).astype(BF16)
    side_ref[0, :, S_KSWA:S_GSWA] = proj(C_KSWA, C_ZSWA).astype(BF16)
    side_ref[0, :, S_GSWA:S_QMEM] = _silu(proj(C_ZSWA, C_QMEM)).astype(BF16)
    side_ref[0, :, S_QMEM:S_GMEM] = (proj(C_QMEM, C_ZMEM) * (MEM_HEAD_DIM ** -0.5)).astype(BF16)
    side_ref[0, :, S_GMEM:SIDE_COLS] = _silu(proj(C_ZMEM, W_IN_COLS)).astype(BF16)


def _proj_call(x, an, win, qn, wuqt, kvn, wuk, eab, wuvt, cost, sint, cc, ss):
    b, s, _ = x.shape
    tm = PROJ_TILE
    ns = s // tm
    in_specs = [
        pl.BlockSpec((1, tm, D_MODEL), lambda bi, i: (bi, i, 0)),
        _const_spec(an.shape), _const_spec(win.shape), _const_spec(qn.shape), _const_spec(wuqt.shape),
        _const_spec(kvn.shape), _const_spec(wuk.shape), _const_spec(eab.shape), _const_spec(wuvt.shape),
        pl.BlockSpec((HALF_ROPE, tm), lambda bi, i: (0, i)),
        pl.BlockSpec((HALF_ROPE, tm), lambda bi, i: (0, i)),
        pl.BlockSpec((tm, LANES), lambda bi, i: (i, 0)),
        pl.BlockSpec((tm, LANES), lambda bi, i: (i, 0)),
    ]
    out_shape = [
        jax.ShapeDtypeStruct((b, MLA_HEADS, ns, MLA_HEAD_PAD, tm), BF16),
        jax.ShapeDtypeStruct((b, MLA_HEADS, s, MLA_HEAD_PAD), BF16),
        jax.ShapeDtypeStruct((b, MLA_HEADS, ns, MLA_V, tm), BF16),
        jax.ShapeDtypeStruct((b, s, MLA_WIDTH), BF16),
        jax.ShapeDtypeStruct((b, s, SIDE_COLS), BF16),
    ]
    out_specs = [
        pl.BlockSpec((1, MLA_HEADS, 1, MLA_HEAD_PAD, tm), lambda bi, i: (bi, 0, i, 0, 0)),
        pl.BlockSpec((1, MLA_HEADS, tm, MLA_HEAD_PAD), lambda bi, i: (bi, 0, i, 0)),
        pl.BlockSpec((1, MLA_HEADS, 1, MLA_V, tm), lambda bi, i: (bi, 0, i, 0, 0)),
        pl.BlockSpec((1, tm, MLA_WIDTH), lambda bi, i: (bi, i, 0)),
        pl.BlockSpec((1, tm, SIDE_COLS), lambda bi, i: (bi, i, 0)),
    ]
    return pl.pallas_call(
        _proj_kernel, grid=(b, ns), in_specs=in_specs, out_specs=out_specs, out_shape=out_shape,
        compiler_params=pltpu.CompilerParams(dimension_semantics=("arbitrary", "arbitrary"),
                                             vmem_limit_bytes=VMEM_LIMIT),
        name="proj",
    )(x, an, win, qn, wuqt, kvn, wuk, eab, wuvt, cost, sint, cc, ss)


def _memkv_kernel(mem_ref, mn_ref, w_ref, o_ref):
    mn = _rms(mem_ref[0], mn_ref[...]).astype(BF16)
    o_ref[0] = jnp.dot(mn, w_ref[...], preferred_element_type=F32).astype(BF16)


def _memkv_call(mem, mn, w):
    b, m, _ = mem.shape
    return pl.pallas_call(
        _memkv_kernel, grid=(b,),
        in_specs=[pl.BlockSpec((1, m, D_MODEL), lambda bi: (bi, 0, 0)), _const_spec(mn.shape), _const_spec(w.shape)],
        out_specs=pl.BlockSpec((1, m, 2 * MEM_WIDTH), lambda bi: (bi, 0, 0)),
        out_shape=jax.ShapeDtypeStruct((b, m, 2 * MEM_WIDTH), BF16),
        compiler_params=pltpu.CompilerParams(dimension_semantics=("arbitrary",), vmem_limit_bytes=VMEM_LIMIT),
        name="memkv",
    )(mem, mn, w)


HEADS_PER_STEP = 2


def _mla_kernel(qt_ref, k_ref, vt_ref, g_ref, o_ref, *, ns):
    t = ATT_TILE
    key_idx = lax.broadcasted_iota(jnp.int32, (t, t), 0)
    qry_idx = lax.broadcasted_iota(jnp.int32, (t, t), 1)
    causal = key_idx <= qry_idx

    def block(hh, q_t, kj, m, l, acc, masked):
        kb = k_ref[0, hh, pl.ds(pl.multiple_of(kj * t, t), t), :]
        st = jnp.dot(kb, q_t, preferred_element_type=F32)
        if masked:
            st = jnp.where(causal, st, NEG)
        m_new = jnp.maximum(m, jnp.max(st, axis=0, keepdims=True))
        p = jnp.exp(st - m_new)
        alpha = jnp.exp(m - m_new)
        l = alpha * l + jnp.sum(p, axis=0, keepdims=True)
        acc = alpha * acc + jnp.dot(vt_ref[0, hh, kj], p.astype(BF16), preferred_element_type=F32)
        return m_new, l, acc

    def q_tile(qi, carry):
        q_ts = [qt_ref[0, hh, qi] for hh in range(HEADS_PER_STEP)]
        init = tuple((jnp.full((1, t), NEG, F32), jnp.zeros((1, t), F32), jnp.zeros((MLA_V, t), F32))
                     for _ in range(HEADS_PER_STEP))

        def body(kj, c):
            return tuple(block(hh, q_ts[hh], kj, *c[hh], False) for hh in range(HEADS_PER_STEP))

        c = lax.fori_loop(0, qi, body, init)
        c = tuple(block(hh, q_ts[hh], qi, *c[hh], True) for hh in range(HEADS_PER_STEP))
        o_t = jnp.concatenate([c[hh][2] / c[hh][1] for hh in range(HEADS_PER_STEP)], axis=0)
        r0 = pl.multiple_of(qi * t, t)
        gate = g_ref[0, pl.ds(r0, t), :].astype(F32)
        o_ref[0, pl.ds(r0, t), :] = (o_t.T * gate).astype(BF16)
        return carry

    lax.fori_loop(0, ns, q_tile, 0)


def _mla_call(qt, k, vt, g):
    b, _, ns, _, t = qt.shape
    s = ns * t
    hp = HEADS_PER_STEP
    return pl.pallas_call(
        functools.partial(_mla_kernel, ns=ns), grid=(b, MLA_HEADS // hp),
        in_specs=[
            pl.BlockSpec((1, hp, ns, MLA_HEAD_PAD, t), lambda bi, h: (bi, h, 0, 0, 0)),
            pl.BlockSpec((1, hp, s, MLA_HEAD_PAD), lambda bi, h: (bi, h, 0, 0)),
            pl.BlockSpec((1, hp, ns, MLA_V, t), lambda bi, h: (bi, h, 0, 0, 0)),
            pl.BlockSpec((1, s, hp * MLA_V), lambda bi, h: (bi, 0, h)),
        ],
        out_specs=pl.BlockSpec((1, s, hp * MLA_V), lambda bi, h: (bi, 0, h)),
        out_shape=jax.ShapeDtypeStruct((b, s, MLA_WIDTH), BF16),
        compiler_params=pltpu.CompilerParams(dimension_semantics=("arbitrary", "arbitrary"),
                                             vmem_limit_bytes=VMEM_LIMIT),
        name="mla_attn",
    )(qt, k, vt, g)


def _side_kernel(sinks_ref, side_ref, prev_ref, kvm_ref, o_ref):
    w = WINDOW
    tq = side_ref.shape[1]
    qi = lax.broadcasted_iota(jnp.int32, (w, 2 * w), 0)
    kj = lax.broadcasted_iota(jnp.int32, (w, 2 * w), 1)
    dist = w + qi - kj
    distf = dist.astype(F32)
    in_window = (dist >= 0) & (dist < w)
    first_mask = in_window & ((kj >= w) | (pl.program_id(1) > 0))
    lane = lax.broadcasted_iota(jnp.int32, (w, LANES), 1)
    lo = lane < SWA_HEAD_DIM
    lane_kv = lax.broadcasted_iota(jnp.int32, (2 * w, LANES), 1)
    lo_kv = lane_kv < SWA_HEAD_DIM

    for n in range(tq // w):
        rows = slice(n * w, (n + 1) * w)
        mask = first_mask if n == 0 else in_window
        for kv in range(SWA_KV_HEADS):
            kcol = slice(S_KSWA + kv * LANES, S_KSWA + (kv + 1) * LANES)
            vcol = slice(S_VSWA + kv * LANES, S_VSWA + (kv + 1) * LANES)
            if n == 0:
                k_prev = prev_ref[0, :, kv * LANES:(kv + 1) * LANES]
                v_prev = prev_ref[0, :, (SWA_KV_HEADS + kv) * LANES:(SWA_KV_HEADS + kv + 1) * LANES]
            else:
                k_prev = side_ref[0, (n - 1) * w:n * w, kcol]
                v_prev = side_ref[0, (n - 1) * w:n * w, vcol]
            kcat = jnp.concatenate([k_prev, side_ref[0, rows, kcol]], axis=0)
            vcat = jnp.concatenate([v_prev, side_ref[0, rows, vcol]], axis=0)
            v_lo = jnp.where(lo_kv, vcat, jnp.zeros_like(vcat))
            v_hi = jnp.where(lo_kv, jnp.zeros_like(vcat), vcat)
            q_parts = []
            for j in range(SWA_GROUP // 2):
                qcol = slice(S_QSWA + (kv * SWA_GROUP + 2 * j) * SWA_HEAD_DIM,
                             S_QSWA + (kv * SWA_GROUP + 2 * j + 2) * SWA_HEAD_DIM)
                qp = side_ref[0, rows, qcol]
                q_parts.append(jnp.where(lo, qp, jnp.zeros_like(qp)))
                q_parts.append(jnp.where(lo, jnp.zeros_like(qp), qp))
            qs = jnp.concatenate(q_parts, axis=0)
            sc = lax.dot_general(qs, kcat, NT_DIMS, preferred_element_type=F32)
            probs = []
            for g in range(SWA_GROUP):
                head = kv * SWA_GROUP + g
                slope = 2.0 ** (-8.0 * (head + 1) / SWA_HEADS)
                sink = sinks_ref[head]
                sg = sc[g * w:(g + 1) * w] - slope * distf
                sg = jnp.where(mask, sg, NEG)
                m = jnp.maximum(jnp.max(sg, axis=-1, keepdims=True), sink)
                e = jnp.exp(sg - m)
                den = jnp.sum(e, axis=-1, keepdims=True) + jnp.exp(sink - m)
                probs.append((e / den).astype(BF16))
            for j in range(SWA_GROUP // 2):
                out = (jnp.dot(probs[2 * j], v_lo, preferred_element_type=F32)
                       + jnp.dot(probs[2 * j + 1], v_hi, preferred_element_type=F32))
                c0 = (kv * SWA_GROUP + 2 * j) * SWA_HEAD_DIM
                gate = side_ref[0, rows, S_GSWA + c0:S_GSWA + c0 + LANES].astype(F32)
                o_ref[0, rows, c0:c0 + LANES] = (out * gate).astype(BF16)

    for h in range(MEM_HEADS):
        hc = slice(h * MEM_HEAD_DIM, (h + 1) * MEM_HEAD_DIM)
        q = side_ref[0, :, S_QMEM + h * MEM_HEAD_DIM:S_QMEM + (h + 1) * MEM_HEAD_DIM]
        sc = lax.dot_general(q, kvm_ref[0, :, hc], NT_DIMS, preferred_element_type=F32)
        m = jnp.max(sc, axis=-1, keepdims=True)
        e = jnp.exp(sc - m)
        p = (e / jnp.sum(e, axis=-1, keepdims=True)).astype(BF16)
        out = jnp.dot(p, kvm_ref[0, :, MEM_WIDTH + h * MEM_HEAD_DIM:MEM_WIDTH + (h + 1) * MEM_HEAD_DIM],
                      preferred_element_type=F32)
        gate = side_ref[0, :, S_GMEM + h * MEM_HEAD_DIM:S_GMEM + (h + 1) * MEM_HEAD_DIM].astype(F32)
        o_ref[0, :, SWA_WIDTH + h * MEM_HEAD_DIM:SWA_WIDTH + (h + 1) * MEM_HEAD_DIM] = (out * gate).astype(BF16)


def _side_call(sinks, side, kvm):
    b, s, _ = side.shape
    tq = SIDE_TILE
    per = tq // WINDOW
    m = kvm.shape[1]
    kv_cols = 2 * 2 * SWA_KV_WIDTH
    return pl.pallas_call(
        _side_kernel, grid=(b, s // tq),
        in_specs=[
            pl.BlockSpec(memory_space=pltpu.SMEM),
            pl.BlockSpec((1, tq, SIDE_COLS), lambda bi, i: (bi, i, 0)),
            pl.BlockSpec((1, WINDOW, kv_cols), lambda bi, i: (bi, jnp.maximum(i * per - 1, 0), S_KSWA // kv_cols)),
            pl.BlockSpec((1, m, 2 * MEM_WIDTH), lambda bi, i: (bi, 0, 0)),
        ],
        out_specs=pl.BlockSpec((1, tq, SWA_WIDTH + MEM_WIDTH), lambda bi, i: (bi, i, 0)),
        out_shape=jax.ShapeDtypeStruct((b, s, SWA_WIDTH + MEM_WIDTH), BF16),
        compiler_params=pltpu.CompilerParams(dimension_semantics=("arbitrary", "arbitrary"),
                                             vmem_limit_bytes=VMEM_LIMIT),
        name="side_attn",
    )(sinks, side, side, kvm)


def _out_kernel(x_ref, ya_ref, yb_ref, w_ref, fn_ref, o_ref, *, final):
    y = (x_ref[0]
         + jnp.dot(ya_ref[0], w_ref[0:MLA_WIDTH, :], preferred_element_type=F32)
         + jnp.dot(yb_ref[0], w_ref[MLA_WIDTH:, :], preferred_element_type=F32))
    if final:
        y = _rms(y, fn_ref[...])
    o_ref[0] = y


def _out_call(x, ya, yb, w, fn, final):
    b, s, _ = x.shape
    tm = OUT_TILE
    return pl.pallas_call(
        functools.partial(_out_kernel, final=final), grid=(b, s // tm),
        in_specs=[
            pl.BlockSpec((1, tm, D_MODEL), lambda bi, i: (bi, i, 0)),
            pl.BlockSpec((1, tm, MLA_WIDTH), lambda bi, i: (bi, i, 0)),
            pl.BlockSpec((1, tm, SWA_WIDTH + MEM_WIDTH), lambda bi, i: (bi, i, 0)),
            _const_spec(w.shape), _const_spec(fn.shape),
        ],
        out_specs=pl.BlockSpec((1, tm, D_MODEL), lambda bi, i: (bi, i, 0)),
        out_shape=jax.ShapeDtypeStruct((b, s, D_MODEL), F32),
        compiler_params=pltpu.CompilerParams(dimension_semantics=("arbitrary", "arbitrary"),
                                             vmem_limit_bytes=VMEM_LIMIT),
        name="out_proj",
    )(x, ya, yb, w, fn)


def _pack_w_in(w):
    c_q, c_kv, k_rope, z_mla, q_swa, k_swa, v_swa, z_swa, q_mem, z_mem = jnp.split(w, SPLIT_IDX, axis=1)
    kr = jnp.pad(k_rope, ((0, 0), (0, LANES - MLA_ROPE)))
    dup = lambda a: jnp.concatenate([a[:, :SWA_HEAD_DIM]] * 2 + [a[:, SWA_HEAD_DIM:]] * 2, axis=1)
    return jnp.concatenate([c_q, c_kv, kr, z_mla, q_swa, dup(k_swa), dup(v_swa), z_swa, q_mem, z_mem],
                           axis=1).astype(BF16)


def _pack_w_uq_t(w):
    w3 = w.reshape(MLA_Q_RANK, MLA_HEADS, MLA_NOPE + MLA_ROPE)
    w3 = jnp.pad(w3, ((0, 0), (0, 0), (0, MLA_HEAD_PAD - MLA_NOPE - MLA_ROPE)))
    return w3.reshape(MLA_Q_RANK, MLA_HEADS * MLA_HEAD_PAD).T.astype(BF16)


def _pack_w_ukv(w):
    w3 = w.reshape(MLA_KV_RANK, MLA_HEADS, MLA_NOPE + MLA_V)
    wk = jnp.pad(w3[:, :, :MLA_NOPE], ((0, 0), (0, 0), (0, MLA_HEAD_PAD - MLA_NOPE)))
    wk = wk.reshape(MLA_KV_RANK, MLA_HEADS * MLA_HEAD_PAD).astype(BF16)
    wvt = w3[:, :, MLA_NOPE:].reshape(MLA_KV_RANK, MLA_WIDTH).T.astype(BF16)
    return wk, wvt


def _rope_placement():
    e = np.zeros((2 * LANES, MLA_HEADS * MLA_HEAD_PAD), np.float32)
    for h in range(MLA_HEADS):
        base = h * MLA_HEAD_PAD + MLA_NOPE
        for i in range(HALF_ROPE):
            e[i, base + i] = 1.0
            e[HALF_ROPE + i, base + HALF_ROPE + i] = 1.0
            e[LANES + i, base + HALF_ROPE + i] = 1.0
            e[LANES + HALF_ROPE + i, base + i] = -1.0
    return jnp.asarray(e, BF16)


def _rope_tables(s):
    inv = ROPE_BASE ** (-jnp.arange(0, MLA_ROPE, 2, dtype=F32) / MLA_ROPE)
    ang = jnp.arange(s, dtype=F32)[:, None] * inv[None, :]
    cos, sin = jnp.cos(ang), jnp.sin(ang)
    pad = jnp.zeros((s, LANES - MLA_ROPE), F32)
    cc = jnp.concatenate([cos, cos, pad], axis=1)
    ss = jnp.concatenate([sin, sin, pad], axis=1)
    return cos.T, sin.T, cc, ss


def kernel(x, mem, attn_norm, w_in, mla_q_norm, w_uq, mla_kv_norm, w_ukv, swa_sinks, mem_norm, w_mem_kv, w_out,
           final_norm):
    depth = w_in.shape[0]
    s = x.shape[1]
    assert s % OUT_TILE == 0 and PROJ_TILE == ATT_TILE and s % SIDE_TILE == 0
    cost, sint, cc, ss = _rope_tables(s)
    eab = _rope_placement()
    fn = final_norm.reshape(1, D_MODEL)
    for l in range(depth):
        wk, wvt = _pack_w_ukv(w_ukv[l])
        qt, k, vt, gmla, side = _proj_call(
            x, attn_norm[l].reshape(1, D_MODEL), _pack_w_in(w_in[l]), mla_q_norm[l].reshape(1, MLA_Q_RANK),
            _pack_w_uq_t(w_uq[l]), mla_kv_norm[l].reshape(1, MLA_KV_RANK), wk, eab, wvt, cost, sint, cc, ss)
        kvm = _memkv_call(mem, mem_norm[l].reshape(1, D_MODEL), w_mem_kv[l].astype(BF16))
        y_mla = _mla_call(qt, k, vt, gmla)
        y_side = _side_call(swa_sinks[l], side, kvm)
        x = _out_call(x, y_mla, y_side, w_out[l].astype(BF16), fn, final=(l == depth - 1))
    return x
```

```python
import functools

import jax
import jax.numpy as jnp
import numpy as np
from jax import lax
from jax.experimental import pallas as pl
from jax.experimental.pallas import tpu as pltpu

F32 = jnp.float32
BF16 = jnp.bfloat16

D_MODEL = 1024
MLA_HEADS = 16
MLA_NOPE = 64
MLA_ROPE = 32
MLA_V = 64
MLA_Q_RANK = 384
MLA_KV_RANK = 256
ROPE_BASE = 10000.0
SWA_HEADS = 8
SWA_KV_HEADS = 2
SWA_HEAD_DIM = 64
SWA_GROUP = SWA_HEADS // SWA_KV_HEADS
WINDOW = 128
MEM_HEADS = 4
MEM_HEAD_DIM = 128
EPS = 1e-6
NEG = -1e30
LOG2_E = 1.4426950408889634

MLA_WIDTH = MLA_HEADS * MLA_V
SWA_WIDTH = SWA_HEADS * SWA_HEAD_DIM
SWA_KV_WIDTH = SWA_KV_HEADS * SWA_HEAD_DIM
MEM_WIDTH = MEM_HEADS * MEM_HEAD_DIM
SPLITS = (MLA_Q_RANK, MLA_KV_RANK, MLA_ROPE, MLA_WIDTH, SWA_WIDTH, SWA_KV_WIDTH, SWA_KV_WIDTH,
          SWA_WIDTH, MEM_WIDTH, MEM_WIDTH)
SPLIT_IDX = tuple(int(i) for i in np.cumsum(SPLITS)[:-1])

LANES = 128
MLA_HEAD_PAD = 128
HALF_ROPE = MLA_ROPE // 2
ATT_TILE = 256
PROJ_TILE = 256
SIDE_TILE = 256
OUT_TILE = 512
VMEM_LIMIT = 48 * 1024 * 1024

C_Q = 0
C_KV = C_Q + MLA_Q_RANK
C_KR = C_KV + MLA_KV_RANK
C_ZMLA = C_KR + LANES
C_QSWA = C_ZMLA + MLA_WIDTH
C_KSWA = C_QSWA + SWA_WIDTH
C_VSWA = C_KSWA + 2 * SWA_KV_WIDTH
C_ZSWA = C_VSWA + 2 * SWA_KV_WIDTH
C_QMEM = C_ZSWA + SWA_WIDTH
C_ZMEM = C_QMEM + MEM_WIDTH
W_IN_COLS = C_ZMEM + MEM_WIDTH

S_QSWA = 0
S_KSWA = S_QSWA + SWA_WIDTH
S_VSWA = S_KSWA + 2 * SWA_KV_WIDTH
S_GSWA = S_VSWA + 2 * SWA_KV_WIDTH
S_QMEM = S_GSWA + SWA_WIDTH
S_GMEM = S_QMEM + MEM_WIDTH
SIDE_COLS = S_GMEM + MEM_WIDTH

NT_DIMS = (((1,), (1,)), ((), ()))


def _rms(x, g):
    return x * lax.rsqrt(jnp.mean(x * x, axis=-1, keepdims=True) + EPS) * g


def _silu(z):
    return z * jax.nn.sigmoid(z)


def _const_spec(shape):
    nd = len(shape)
    return pl.BlockSpec(shape, lambda *_: (0,) * nd, pipeline_mode=pl.Buffered(1))


def _proj_kernel(x_ref, an_ref, win_ref, qn_ref, wuqt_ref, kvn_ref, wuk_ref, eab_ref, wuvt_ref,
                 cost_ref, sint_ref, cc_ref, ss_ref,
                 qt_ref, k_ref, vt_ref, gmla_ref, side_ref):
    hn = _rms(x_ref[0], an_ref[...]).astype(BF16)

    def proj(a, b):
        return jnp.dot(hn, win_ref[:, a:b], preferred_element_type=F32)

    cqn = _rms(proj(C_Q, C_KV), qn_ref[...]).astype(BF16)
    scale = (MLA_NOPE + MLA_ROPE) ** -0.5 * LOG2_E
    qt = lax.dot_general(wuqt_ref[...], cqn, NT_DIMS, preferred_element_type=F32) * scale
    c = cost_ref[...]
    s = sint_ref[...]
    zero_rows = jnp.zeros((MLA_HEAD_PAD - MLA_NOPE - MLA_ROPE, qt.shape[1]), BF16)
    for h in range(MLA_HEADS):
        r = h * MLA_HEAD_PAD
        x1 = qt[r + MLA_NOPE:r + MLA_NOPE + HALF_ROPE]
        x2 = qt[r + MLA_NOPE + HALF_ROPE:r + MLA_NOPE + MLA_ROPE]
        qt_ref[0, h, 0, 0:MLA_NOPE, :] = qt[r:r + MLA_NOPE].astype(BF16)
        qt_ref[0, h, 0, MLA_NOPE:MLA_NOPE + HALF_ROPE, :] = (x1 * c - x2 * s).astype(BF16)
        qt_ref[0, h, 0, MLA_NOPE + HALF_ROPE:MLA_NOPE + MLA_ROPE, :] = (x1 * s + x2 * c).astype(BF16)
        qt_ref[0, h, 0, MLA_NOPE + MLA_ROPE:MLA_HEAD_PAD, :] = zero_rows

    ckvn = _rms(proj(C_KV, C_KR), kvn_ref[...]).astype(BF16)
    kr = proj(C_KR, C_ZMLA)
    krab = jnp.concatenate([kr * cc_ref[...], kr * ss_ref[...]], axis=1).astype(BF16)
    kk = (jnp.dot(ckvn, wuk_ref[...], preferred_element_type=F32)
          + jnp.dot(krab, eab_ref[...], preferred_element_type=F32))
    for h in range(MLA_HEADS):
        k_ref[0, h, :, :] = kk[:, h * MLA_HEAD_PAD:(h + 1) * MLA_HEAD_PAD].astype(BF16)

    vt = lax.dot_general(wuvt_ref[...], ckvn, NT_DIMS, preferred_element_type=F32)
    for h in range(MLA_HEADS):
        vt_ref[0, h, 0, :, :] = vt[h * MLA_V:(h + 1) * MLA_V].astype(BF16)

    gmla_ref[0] = _silu(proj(C_ZMLA, C_QSWA)).astype(BF16)
    side_ref[0, :, S_QSWA:S_KSWA] = (proj(C_QSWA, C_KSWA) * (SWA_HEAD_DIM ** -0.5)).astype(BF16)
    side_ref[0, :, S_KSWA:S_GSWA] = proj(C_KSWA, C_ZSWA).astype(BF16)
    side_ref[0, :, S_GSWA:S_QMEM] = _silu(proj(C_ZSWA, C_QMEM)).astype(BF16)
    side_ref[0, :, S_QMEM:S_GMEM] = (proj(C_QMEM, C_ZMEM) * (MEM_HEAD_DIM ** -0.5)).astype(BF16)
    side_ref[0, :, S_GMEM:SIDE_COLS] = _silu(proj(C_ZMEM, W_IN_COLS)).astype(BF16)


def _proj_call(x, an, win, qn, wuqt, kvn, wuk, eab, wuvt, cost, sint, cc, ss):
    b, s, _ = x.shape
    tm = PROJ_TILE
    ns = s // tm
    in_specs = [
        pl.BlockSpec((1, tm, D_MODEL), lambda bi, i: (bi, i, 0)),
        _const_spec(an.shape), _const_spec(win.shape), _const_spec(qn.shape), _const_spec(wuqt.shape),
        _const_spec(kvn.shape), _const_spec(wuk.shape), _const_spec(eab.shape), _const_spec(wuvt.shape),
        pl.BlockSpec((HALF_ROPE, tm), lambda bi, i: (0, i)),
        pl.BlockSpec((HALF_ROPE, tm), lambda bi, i: (0, i)),
        pl.BlockSpec((tm, LANES), lambda bi, i: (i, 0)),
        pl.BlockSpec((tm, LANES), lambda bi, i: (i, 0)),
    ]
    out_shape = [
        jax.ShapeDtypeStruct((b, MLA_HEADS, ns, MLA_HEAD_PAD, tm), BF16),
        jax.ShapeDtypeStruct((b, MLA_HEADS, s, MLA_HEAD_PAD), BF16),
        jax.ShapeDtypeStruct((b, MLA_HEADS, ns, MLA_V, tm), BF16),
        jax.ShapeDtypeStruct((b, s, MLA_WIDTH), BF16),
        jax.ShapeDtypeStruct((b, s, SIDE_COLS), BF16),
    ]
    out_specs = [
        pl.BlockSpec((1, MLA_HEADS, 1, MLA_HEAD_PAD, tm), lambda bi, i: (bi, 0, i, 0, 0)),
        pl.BlockSpec((1, MLA_HEADS, tm, MLA_HEAD_PAD), lambda bi, i: (bi, 0, i, 0)),
        pl.BlockSpec((1, MLA_HEADS, 1, MLA_V, tm), lambda bi, i: (bi, 0, i, 0, 0)),
        pl.BlockSpec((1, tm, MLA_WIDTH), lambda bi, i: (bi, i, 0)),
        pl.BlockSpec((1, tm, SIDE_COLS), lambda bi, i: (bi, i, 0)),
    ]
    return pl.pallas_call(
        _proj_kernel, grid=(b, ns), in_specs=in_specs, out_specs=out_specs, out_shape=out_shape,
        compiler_params=pltpu.CompilerParams(dimension_semantics=("arbitrary", "arbitrary"),
                                             vmem_limit_bytes=VMEM_LIMIT),
        name="proj",
    )(x, an, win, qn, wuqt, kvn, wuk, eab, wuvt, cost, sint, cc, ss)


def _memkv_kernel(mem_ref, mn_ref, w_ref, o_ref):
    mn = _rms(mem_ref[0], mn_ref[...]).astype(BF16)
    o_ref[0] = jnp.dot(mn, w_ref[...], preferred_element_type=F32).astype(BF16)


def _memkv_call(mem, mn, w):
    b, m, _ = mem.shape
    return pl.pallas_call(
        _memkv_kernel, grid=(b,),
        in_specs=[pl.BlockSpec((1, m, D_MODEL), lambda bi: (bi, 0, 0)), _const_spec(mn.shape), _const_spec(w.shape)],
        out_specs=pl.BlockSpec((1, m, 2 * MEM_WIDTH), lambda bi: (bi, 0, 0)),
        out_shape=jax.ShapeDtypeStruct((b, m, 2 * MEM_WIDTH), BF16),
        compiler_params=pltpu.CompilerParams(dimension_semantics=("arbitrary",), vmem_limit_bytes=VMEM_LIMIT),
        name="memkv",
    )(mem, mn, w)


HEADS_PER_STEP = 2


def _mla_kernel(qt_ref, k_ref, vt_ref, g_ref, o_ref, s_scr, m_scr, l_scr, acc_scr, *, ns):
    t = ATT_TILE
    heads = range(HEADS_PER_STEP)
    key_idx = lax.broadcasted_iota(jnp.int32, (t, t), 0)
    qry_idx = lax.broadcasted_iota(jnp.int32, (t, t), 1)
    causal = key_idx <= qry_idx

    def q_tile(qi, carry):
        q_ts = [qt_ref[0, hh, qi] for hh in heads]
        for hh in heads:
            m_scr[hh] = jnp.full((1, t), NEG, F32)
            l_scr[hh] = jnp.zeros((1, t), F32)
            acc_scr[hh] = jnp.zeros((MLA_V, t), F32)

        def scores(kj, masked):
            for hh in heads:
                kb = k_ref[0, hh, pl.ds(pl.multiple_of(kj * t, t), t), :]
                st = jnp.dot(kb, q_ts[hh], preferred_element_type=F32)
                if masked:
                    st = jnp.where(causal, st, NEG)
                s_scr[hh, kj] = st
                m_scr[hh] = jnp.maximum(m_scr[hh], jnp.max(st, axis=0, keepdims=True))

        def scores_pair(i, c):
            scores(2 * i, False)
            scores(2 * i + 1, False)
            return c

        lax.fori_loop(0, qi // 2, scores_pair, 0)

        @pl.when(qi % 2 == 1)
        def _():
            scores(qi - 1, False)

        scores(qi, True)
        m_fin = [m_scr[hh] for hh in heads]

        def weights(kj):
            for hh in heads:
                p = jnp.exp2(s_scr[hh, kj] - m_fin[hh])
                l_scr[hh] += jnp.sum(p, axis=0, keepdims=True)
                acc_scr[hh] += jnp.dot(vt_ref[0, hh, kj], p.astype(BF16), preferred_element_type=F32)

        def weights_pair(i, c):
            weights(2 * i)
            weights(2 * i + 1)
            return c

        lax.fori_loop(0, (qi + 1) // 2, weights_pair, 0)

        @pl.when(qi % 2 == 0)
        def _():
            weights(qi)

        o_t = jnp.concatenate([acc_scr[hh] / l_scr[hh] for hh in heads], axis=0)
        r0 = pl.multiple_of(qi * t, t)
        gate = g_ref[0, pl.ds(r0, t), :].astype(F32)
        o_ref[0, pl.ds(r0, t), :] = (o_t.T * gate).astype(BF16)
        return carry

    lax.fori_loop(0, ns, q_tile, 0)


def _mla_call(qt, k, vt, g):
    b, _, ns, _, t = qt.shape
    s = ns * t
    hp = HEADS_PER_STEP
    return pl.pallas_call(
        functools.partial(_mla_kernel, ns=ns), grid=(b, MLA_HEADS // hp),
        in_specs=[
            pl.BlockSpec((1, hp, ns, MLA_HEAD_PAD, t), lambda bi, h: (bi, h, 0, 0, 0)),
            pl.BlockSpec((1, hp, s, MLA_HEAD_PAD), lambda bi, h: (bi, h, 0, 0)),
            pl.BlockSpec((1, hp, ns, MLA_V, t), lambda bi, h: (bi, h, 0, 0, 0)),
            pl.BlockSpec((1, s, hp * MLA_V), lambda bi, h: (bi, 0, h)),
        ],
        out_specs=pl.BlockSpec((1, s, hp * MLA_V), lambda bi, h: (bi, 0, h)),
        out_shape=jax.ShapeDtypeStruct((b, s, MLA_WIDTH), BF16),
        scratch_shapes=[
            pltpu.VMEM((hp, ns, t, t), F32),
            pltpu.VMEM((hp, 1, t), F32),
            pltpu.VMEM((hp, 1, t), F32),
            pltpu.VMEM((hp, MLA_V, t), F32),
        ],
        compiler_params=pltpu.CompilerParams(dimension_semantics=("arbitrary", "arbitrary"),
                                             vmem_limit_bytes=VMEM_LIMIT),
        name="mla_attn",
    )(qt, k, vt, g)


def _side_kernel(sinks_ref, side_ref, prev_ref, kvm_ref, o_ref):
    w = WINDOW
    tq = side_ref.shape[1]
    qi = lax.broadcasted_iota(jnp.int32, (w, 2 * w), 0)
    kj = lax.broadcasted_iota(jnp.int32, (w, 2 * w), 1)
    dist = w + qi - kj
    distf = dist.astype(F32)
    in_window = (dist >= 0) & (dist < w)
    first_mask = in_window & ((kj >= w) | (pl.program_id(1) > 0))
    lane = lax.broadcasted_iota(jnp.int32, (w, LANES), 1)
    lo = lane < SWA_HEAD_DIM
    lane_kv = lax.broadcasted_iota(jnp.int32, (2 * w, LANES), 1)
    lo_kv = lane_kv < SWA_HEAD_DIM

    for n in range(tq // w):
        rows = slice(n * w, (n + 1) * w)
        mask = first_mask if n == 0 else in_window
        for kv in range(SWA_KV_HEADS):
            kcol = slice(S_KSWA + kv * LANES, S_KSWA + (kv + 1) * LANES)
            vcol = slice(S_VSWA + kv * LANES, S_VSWA + (kv + 1) * LANES)
            if n == 0:
                k_prev = prev_ref[0, :, kv * LANES:(kv + 1) * LANES]
                v_prev = prev_ref[0, :, (SWA_KV_HEADS + kv) * LANES:(SWA_KV_HEADS + kv + 1) * LANES]
            else:
                k_prev = side_ref[0, (n - 1) * w:n * w, kcol]
                v_prev = side_ref[0, (n - 1) * w:n * w, vcol]
            kcat = jnp.concatenate([k_prev, side_ref[0, rows, kcol]], axis=0)
            vcat = jnp.concatenate([v_prev, side_ref[0, rows, vcol]], axis=0)
            v_lo = jnp.where(lo_kv, vcat, jnp.zeros_like(vcat))
            v_hi = jnp.where(lo_kv, jnp.zeros_like(vcat), vcat)
            q_parts = []
            for j in range(SWA_GROUP // 2):
                qcol = slice(S_QSWA + (kv * SWA_GROUP + 2 * j) * SWA_HEAD_DIM,
                             S_QSWA + (kv * SWA_GROUP + 2 * j + 2) * SWA_HEAD_DIM)
                qp = side_ref[0, rows, qcol]
                q_parts.append(jnp.where(lo, qp, jnp.zeros_like(qp)))
                q_parts.append(jnp.where(lo, jnp.zeros_like(qp), qp))
            qs = jnp.concatenate(q_parts, axis=0)
            sc = lax.dot_general(qs, kcat, NT_DIMS, preferred_element_type=F32)
            probs = []
            for g in range(SWA_GROUP):
                head = kv * SWA_GROUP + g
                slope = 2.0 ** (-8.0 * (head + 1) / SWA_HEADS)
                sink = sinks_ref[head]
                sg = sc[g * w:(g + 1) * w] - slope * distf
                sg = jnp.where(mask, sg, NEG)
                m = jnp.maximum(jnp.max(sg, axis=-1, keepdims=True), sink)
                e = jnp.exp(sg - m)
                den = jnp.sum(e, axis=-1, keepdims=True) + jnp.exp(sink - m)
                probs.append((e / den).astype(BF16))
            for j in range(SWA_GROUP // 2):
                out = (jnp.dot(probs[2 * j], v_lo, preferred_element_type=F32)
                       + jnp.dot(probs[2 * j + 1], v_hi, preferred_element_type=F32))
                c0 = (kv * SWA_GROUP + 2 * j) * SWA_HEAD_DIM
                gate = side_ref[0, rows, S_GSWA + c0:S_GSWA + c0 + LANES].astype(F32)
                o_ref[0, rows, c0:c0 + LANES] = (out * gate).astype(BF16)

    for h in range(MEM_HEADS):
        hc = slice(h * MEM_HEAD_DIM, (h + 1) * MEM_HEAD_DIM)
        q = side_ref[0, :, S_QMEM + h * MEM_HEAD_DIM:S_QMEM + (h + 1) * MEM_HEAD_DIM]
        sc = lax.dot_general(q, kvm_ref[0, :, hc], NT_DIMS, preferred_element_type=F32)
        m = jnp.max(sc, axis=-1, keepdims=True)
        e = jnp.exp(sc - m)
        p = (e / jnp.sum(e, axis=-1, keepdims=True)).astype(BF16)
        out = jnp.dot(p, kvm_ref[0, :, MEM_WIDTH + h * MEM_HEAD_DIM:MEM_WIDTH + (h + 1) * MEM_HEAD_DIM],
                      preferred_element_type=F32)
        gate = side_ref[0, :, S_GMEM + h * MEM_HEAD_DIM:S_GMEM + (h + 1) * MEM_HEAD_DIM].astype(F32)
        o_ref[0, :, SWA_WIDTH + h * MEM_HEAD_DIM:SWA_WIDTH + (h + 1) * MEM_HEAD_DIM] = (out * gate).astype(BF16)


def _side_call(sinks, side, kvm):
    b, s, _ = side.shape
    tq = SIDE_TILE
    per = tq // WINDOW
    m = kvm.shape[1]
    kv_cols = 2 * 2 * SWA_KV_WIDTH
    return pl.pallas_call(
        _side_kernel, grid=(b, s // tq),
        in_specs=[
            pl.BlockSpec(memory_space=pltpu.SMEM),
            pl.BlockSpec((1, tq, SIDE_COLS), lambda bi, i: (bi, i, 0)),
            pl.BlockSpec((1, WINDOW, kv_cols), lambda bi, i: (bi, jnp.maximum(i * per - 1, 0), S_KSWA // kv_cols)),
            pl.BlockSpec((1, m, 2 * MEM_WIDTH), lambda bi, i: (bi, 0, 0)),
        ],
        out_specs=pl.BlockSpec((1, tq, SWA_WIDTH + MEM_WIDTH), lambda bi, i: (bi, i, 0)),
        out_shape=jax.ShapeDtypeStruct((b, s, SWA_WIDTH + MEM_WIDTH), BF16),
        compiler_params=pltpu.CompilerParams(dimension_semantics=("arbitrary", "arbitrary"),
                                             vmem_limit_bytes=VMEM_LIMIT),
        name="side_attn",
    )(sinks, side, side, kvm)


def _out_kernel(x_ref, ya_ref, yb_ref, w_ref, fn_ref, o_ref, *, final):
    y = (x_ref[0]
         + jnp.dot(ya_ref[0], w_ref[0:MLA_WIDTH, :], preferred_element_type=F32)
         + jnp.dot(yb_ref[0], w_ref[MLA_WIDTH:, :], preferred_element_type=F32))
    if final:
        y = _rms(y, fn_ref[...])
    o_ref[0] = y


def _out_call(x, ya, yb, w, fn, final):
    b, s, _ = x.shape
    tm = OUT_TILE
    return pl.pallas_call(
        functools.partial(_out_kernel, final=final), grid=(b, s // tm),
        in_specs=[
            pl.BlockSpec((1, tm, D_MODEL), lambda bi, i: (bi, i, 0)),
            pl.BlockSpec((1, tm, MLA_WIDTH), lambda bi, i: (bi, i, 0)),
            pl.BlockSpec((1, tm, SWA_WIDTH + MEM_WIDTH), lambda bi, i: (bi, i, 0)),
            _const_spec(w.shape), _const_spec(fn.shape),
        ],
        out_specs=pl.BlockSpec((1, tm, D_MODEL), lambda bi, i: (bi, i, 0)),
        out_shape=jax.ShapeDtypeStruct((b, s, D_MODEL), F32),
        compiler_params=pltpu.CompilerParams(dimension_semantics=("arbitrary", "arbitrary"),
                                             vmem_limit_bytes=VMEM_LIMIT),
        name="out_proj",
    )(x, ya, yb, w, fn)


def _pack_w_in(w):
    c_q, c_kv, k_rope, z_mla, q_swa, k_swa, v_swa, z_swa, q_mem, z_mem = jnp.split(w, SPLIT_IDX, axis=1)
    kr = jnp.pad(k_rope, ((0, 0), (0, LANES - MLA_ROPE)))
    dup = lambda a: jnp.concatenate([a[:, :SWA_HEAD_DIM]] * 2 + [a[:, SWA_HEAD_DIM:]] * 2, axis=1)
    return jnp.concatenate([c_q, c_kv, kr, z_mla, q_swa, dup(k_swa), dup(v_swa), z_swa, q_mem, z_mem],
                           axis=1).astype(BF16)


def _pack_w_uq_t(w):
    w3 = w.reshape(MLA_Q_RANK, MLA_HEADS, MLA_NOPE + MLA_ROPE)
    w3 = jnp.pad(w3, ((0, 0), (0, 0), (0, MLA_HEAD_PAD - MLA_NOPE - MLA_ROPE)))
    return w3.reshape(MLA_Q_RANK, MLA_HEADS * MLA_HEAD_PAD).T.astype(BF16)


def _pack_w_ukv(w):
    w3 = w.reshape(MLA_KV_RANK, MLA_HEADS, MLA_NOPE + MLA_V)
    wk = jnp.pad(w3[:, :, :MLA_NOPE], ((0, 0), (0, 0), (0, MLA_HEAD_PAD - MLA_NOPE)))
    wk = wk.reshape(MLA_KV_RANK, MLA_HEADS * MLA_HEAD_PAD).astype(BF16)
    wvt = w3[:, :, MLA_NOPE:].reshape(MLA_KV_RANK, MLA_WIDTH).T.astype(BF16)
    return wk, wvt


def _rope_placement():
    e = np.zeros((2 * LANES, MLA_HEADS * MLA_HEAD_PAD), np.float32)
    for h in range(MLA_HEADS):
        base = h * MLA_HEAD_PAD + MLA_NOPE
        for i in range(HALF_ROPE):
            e[i, base + i] = 1.0
            e[HALF_ROPE + i, base + HALF_ROPE + i] = 1.0
            e[LANES + i, base + HALF_ROPE + i] = 1.0
            e[LANES + HALF_ROPE + i, base + i] = -1.0
    return jnp.asarray(e, BF16)


def _rope_tables(s):
    inv = ROPE_BASE ** (-jnp.arange(0, MLA_ROPE, 2, dtype=F32) / MLA_ROPE)
    ang = jnp.arange(s, dtype=F32)[:, None] * inv[None, :]
    cos, sin = jnp.cos(ang), jnp.sin(ang)
    pad = jnp.zeros((s, LANES - MLA_ROPE), F32)
    cc = jnp.concatenate([cos, cos, pad], axis=1)
    ss = jnp.concatenate([sin, sin, pad], axis=1)
    return cos.T, sin.T, cc, ss


def kernel(x, mem, attn_norm, w_in, mla_q_norm, w_uq, mla_kv_norm, w_ukv, swa_sinks, mem_norm, w_mem_kv, w_out,
           final_norm):
    depth = w_in.shape[0]
    s = x.shape[1]
    assert s % OUT_TILE == 0 and PROJ_TILE == ATT_TILE and s % SIDE_TILE == 0
    cost, sint, cc, ss = _rope_tables(s)
    eab = _rope_placement()
    fn = final_norm.reshape(1, D_MODEL)
    for l in range(depth):
        wk, wvt = _pack_w_ukv(w_ukv[l])
        qt, k, vt, gmla, side = _proj_call(
            x, attn_norm[l].reshape(1, D_MODEL), _pack_w_in(w_in[l]), mla_q_norm[l].reshape(1, MLA_Q_RANK),
            _pack_w_uq_t(w_uq[l]), mla_kv_norm[l].reshape(1, MLA_KV_RANK), wk, eab, wvt, cost, sint, cc, ss)
        kvm = _memkv_call(mem, mem_norm[l].reshape(1, D_MODEL), w_mem_kv[l].astype(BF16))
        y_mla = _mla_call(qt, k, vt, gmla)
        y_side = _side_call(swa_sinks[l], side, kvm)
        x = _out_call(x, y_mla, y_side, w_out[l].astype(BF16), fn, final=(l == depth - 1))
    return x
```

```python
import functools

import jax
import jax.numpy as jnp
import numpy as np
from jax import lax
from jax.experimental import pallas as pl
from jax.experimental.pallas import tpu as pltpu

F32 = jnp.float32
BF16 = jnp.bfloat16

D_MODEL = 1024
MLA_HEADS = 16
MLA_NOPE = 64
MLA_ROPE = 32
MLA_V = 64
MLA_Q_RANK = 384
MLA_KV_RANK = 256
ROPE_BASE = 10000.0
SWA_HEADS = 8
SWA_KV_HEADS = 2
SWA_HEAD_DIM = 64
SWA_GROUP = SWA_HEADS // SWA_KV_HEADS
WINDOW = 128
MEM_HEADS = 4
MEM_HEAD_DIM = 128
EPS = 1e-6
NEG = -1e30
LOG2_E = 1.4426950408889634

MLA_WIDTH = MLA_HEADS * MLA_V
SWA_WIDTH = SWA_HEADS * SWA_HEAD_DIM
SWA_KV_WIDTH = SWA_KV_HEADS * SWA_HEAD_DIM
MEM_WIDTH = MEM_HEADS * MEM_HEAD_DIM
SPLITS = (MLA_Q_RANK, MLA_KV_RANK, MLA_ROPE, MLA_WIDTH, SWA_WIDTH, SWA_KV_WIDTH, SWA_KV_WIDTH,
          SWA_WIDTH, MEM_WIDTH, MEM_WIDTH)
SPLIT_IDX = tuple(int(i) for i in np.cumsum(SPLITS)[:-1])

LANES = 128
MLA_HEAD_PAD = 128
HALF_ROPE = MLA_ROPE // 2
ATT_TILE = 256
PROJ_TILE = 256
SIDE_TILE = 256
OUT_TILE = 512
VMEM_LIMIT = 48 * 1024 * 1024

C_Q = 0
C_KV = C_Q + MLA_Q_RANK
C_KR = C_KV + MLA_KV_RANK
C_ZMLA = C_KR + LANES
C_QSWA = C_ZMLA + MLA_WIDTH
C_KSWA = C_QSWA + SWA_WIDTH
C_VSWA = C_KSWA + 2 * SWA_KV_WIDTH
C_ZSWA = C_VSWA + 2 * SWA_KV_WIDTH
C_QMEM = C_ZSWA + SWA_WIDTH
C_ZMEM = C_QMEM + MEM_WIDTH
W_IN_COLS = C_ZMEM + MEM_WIDTH

S_QSWA = 0
S_KSWA = S_QSWA + SWA_WIDTH
S_VSWA = S_KSWA + 2 * SWA_KV_WIDTH
S_GSWA = S_VSWA + 2 * SWA_KV_WIDTH
S_QMEM = S_GSWA + SWA_WIDTH
S_GMEM = S_QMEM + MEM_WIDTH
SIDE_COLS = S_GMEM + MEM_WIDTH

NT_DIMS = (((1,), (1,)), ((), ()))


def _rms(x, g):
    return x * lax.rsqrt(jnp.mean(x * x, axis=-1, keepdims=True) + EPS) * g


def _silu(z):
    return z * jax.nn.sigmoid(z)


def _const_spec(shape):
    nd = len(shape)
    return pl.BlockSpec(shape, lambda *_: (0,) * nd, pipeline_mode=pl.Buffered(1))


def _proj_kernel(x_ref, an_ref, win_ref, qn_ref, wuqt_ref, kvn_ref, wuk_ref, wuvt_ref,
                 cost_ref, sint_ref, cc_ref, sa_ref, sb_ref,
                 qt_ref, k_ref, vt_ref, gmla_ref, side_ref):
    hn = _rms(x_ref[0], an_ref[...]).astype(BF16)

    def proj(a, b):
        return jnp.dot(hn, win_ref[:, a:b], preferred_element_type=F32)

    cqn = _rms(proj(C_Q, C_KV), qn_ref[...]).astype(BF16)
    scale = (MLA_NOPE + MLA_ROPE) ** -0.5 * LOG2_E
    qt = lax.dot_general(wuqt_ref[...], cqn, NT_DIMS, preferred_element_type=F32) * scale
    c = cost_ref[...]
    s = sint_ref[...]
    zero_rows = jnp.zeros((MLA_HEAD_PAD - MLA_NOPE - MLA_ROPE, qt.shape[1]), BF16)
    for h in range(MLA_HEADS):
        r = h * MLA_HEAD_PAD
        x1 = qt[r + MLA_NOPE:r + MLA_NOPE + HALF_ROPE]
        x2 = qt[r + MLA_NOPE + HALF_ROPE:r + MLA_NOPE + MLA_ROPE]
        qt_ref[0, h, 0, 0:MLA_NOPE, :] = qt[r:r + MLA_NOPE].astype(BF16)
        qt_ref[0, h, 0, MLA_NOPE:MLA_NOPE + HALF_ROPE, :] = (x1 * c - x2 * s).astype(BF16)
        qt_ref[0, h, 0, MLA_NOPE + HALF_ROPE:MLA_NOPE + MLA_ROPE, :] = (x1 * s + x2 * c).astype(BF16)
        qt_ref[0, h, 0, MLA_NOPE + MLA_ROPE:MLA_HEAD_PAD, :] = zero_rows

    ckvn = _rms(proj(C_KV, C_KR), kvn_ref[...]).astype(BF16)
    kr = proj(C_KR, C_ZMLA)
    kpe = (kr * cc_ref[...] + pltpu.roll(kr, LANES - HALF_ROPE, 1) * sa_ref[...]
           + pltpu.roll(kr, HALF_ROPE, 1) * sb_ref[...])
    kk = jnp.dot(ckvn, wuk_ref[...], preferred_element_type=F32)
    for h in range(MLA_HEADS):
        k_ref[0, h, :, :] = (kk[:, h * MLA_HEAD_PAD:(h + 1) * MLA_HEAD_PAD] + kpe).astype(BF16)

    vt = lax.dot_general(wuvt_ref[...], ckvn, NT_DIMS, preferred_element_type=F32)
    for h in range(MLA_HEADS):
        vt_ref[0, h, 0, :, :] = vt[h * MLA_V:(h + 1) * MLA_V].astype(BF16)

    gmla_ref[0] = _silu(proj(C_ZMLA, C_QSWA)).astype(BF16)
    side_ref[0, :, S_QSWA:S_KSWA] = (proj(C_QSWA, C_KSWA) * (SWA_HEAD_DIM ** -0.5)).astype(BF16)
    side_ref[0, :, S_KSWA:S_GSWA] = proj(C_KSWA, C_ZSWA).astype(BF16)
    side_ref[0, :, S_GSWA:S_QMEM] = _silu(proj(C_ZSWA, C_QMEM)).astype(BF16)
    side_ref[0, :, S_QMEM:S_GMEM] = (proj(C_QMEM, C_ZMEM) * (MEM_HEAD_DIM ** -0.5)).astype(BF16)
    side_ref[0, :, S_GMEM:SIDE_COLS] = _silu(proj(C_ZMEM, W_IN_COLS)).astype(BF16)


def _proj_call(x, an, win, qn, wuqt, kvn, wuk, wuvt, cost, sint, cc, sa, sb):
    b, s, _ = x.shape
    tm = PROJ_TILE
    ns = s // tm
    in_specs = [
        pl.BlockSpec((1, tm, D_MODEL), lambda bi, i: (bi, i, 0)),
        _const_spec(an.shape), _const_spec(win.shape), _const_spec(qn.shape), _const_spec(wuqt.shape),
        _const_spec(kvn.shape), _const_spec(wuk.shape), _const_spec(wuvt.shape),
        pl.BlockSpec((HALF_ROPE, tm), lambda bi, i: (0, i)),
        pl.BlockSpec((HALF_ROPE, tm), lambda bi, i: (0, i)),
        pl.BlockSpec((tm, LANES), lambda bi, i: (i, 0)),
        pl.BlockSpec((tm, LANES), lambda bi, i: (i, 0)),
        pl.BlockSpec((tm, LANES), lambda bi, i: (i, 0)),
    ]
    out_shape = [
        jax.ShapeDtypeStruct((b, MLA_HEADS, ns, MLA_HEAD_PAD, tm), BF16),
        jax.ShapeDtypeStruct((b, MLA_HEADS, s, MLA_HEAD_PAD), BF16),
        jax.ShapeDtypeStruct((b, MLA_HEADS, ns, MLA_V, tm), BF16),
        jax.ShapeDtypeStruct((b, s, MLA_WIDTH), BF16),
        jax.ShapeDtypeStruct((b, s, SIDE_COLS), BF16),
    ]
    out_specs = [
        pl.BlockSpec((1, MLA_HEADS, 1, MLA_HEAD_PAD, tm), lambda bi, i: (bi, 0, i, 0, 0)),
        pl.BlockSpec((1, MLA_HEADS, tm, MLA_HEAD_PAD), lambda bi, i: (bi, 0, i, 0)),
        pl.BlockSpec((1, MLA_HEADS, 1, MLA_V, tm), lambda bi, i: (bi, 0, i, 0, 0)),
        pl.BlockSpec((1, tm, MLA_WIDTH), lambda bi, i: (bi, i, 0)),
        pl.BlockSpec((1, tm, SIDE_COLS), lambda bi, i: (bi, i, 0)),
    ]
    return pl.pallas_call(
        _proj_kernel, grid=(b, ns), in_specs=in_specs, out_specs=out_specs, out_shape=out_shape,
        compiler_params=pltpu.CompilerParams(dimension_semantics=("arbitrary", "arbitrary"),
                                             vmem_limit_bytes=VMEM_LIMIT),
        name="proj",
    )(x, an, win, qn, wuqt, kvn, wuk, wuvt, cost, sint, cc, sa, sb)


def _memkv_kernel(mem_ref, mn_ref, w_ref, o_ref):
    mn = _rms(mem_ref[0], mn_ref[...]).astype(BF16)
    o_ref[0] = jnp.dot(mn, w_ref[...], preferred_element_type=F32).astype(BF16)


def _memkv_call(mem, mn, w):
    b, m, _ = mem.shape
    return pl.pallas_call(
        _memkv_kernel, grid=(b,),
        in_specs=[pl.BlockSpec((1, m, D_MODEL), lambda bi: (bi, 0, 0)), _const_spec(mn.shape), _const_spec(w.shape)],
        out_specs=pl.BlockSpec((1, m, 2 * MEM_WIDTH), lambda bi: (bi, 0, 0)),
        out_shape=jax.ShapeDtypeStruct((b, m, 2 * MEM_WIDTH), BF16),
        compiler_params=pltpu.CompilerParams(dimension_semantics=("arbitrary",), vmem_limit_bytes=VMEM_LIMIT),
        name="memkv",
    )(mem, mn, w)


HEADS_PER_STEP = 2
MAX_SLABS = 4
MLA_UNROLL = 4


def _mla_kernel(qt_ref, k_ref, vt_ref, g_ref, o_ref, s0, s1, p0, p1, m_scr, l_scr, acc_scr, *, ns):
    t = ATT_TILE
    heads = range(HEADS_PER_STEP)
    unroll = MLA_UNROLL
    s_scr = (s0, s1)
    p_scr = (p0, p1)
    key_idx = lax.broadcasted_iota(jnp.int32, (t, t), 0)
    qry_idx = lax.broadcasted_iota(jnp.int32, (t, t), 1)
    causal = key_idx <= qry_idx

    def scores(par, tile, kj, masked):
        for hh in heads:
            kb = k_ref[0, hh, pl.ds(pl.multiple_of(kj * t, t), t), :]
            st = jnp.dot(kb, qt_ref[0, hh, tile], preferred_element_type=F32)
            if masked:
                st = jnp.where(causal, st, NEG)
            s_scr[par][hh, kj] = st
            slab = jnp.max(st.reshape(MAX_SLABS, t // MAX_SLABS, t), axis=0)
            m_scr[par, hh] = jnp.maximum(m_scr[par, hh], jnp.max(slab, axis=0, keepdims=True))

    def exps(par, m_fin, kj):
        for hh in heads:
            p = jnp.exp2(s_scr[par][hh, kj] - m_fin[hh])
            l_scr[par, hh] += jnp.sum(p, axis=0, keepdims=True)
            p_scr[par][hh, kj] = p.astype(BF16)

    def pv(par, kj):
        for hh in heads:
            acc_scr[hh] += jnp.dot(vt_ref[0, hh, kj], p_scr[par][hh, kj], preferred_element_type=F32)

    def start_scores(par):
        for hh in heads:
            m_scr[par, hh] = jnp.full((1, t), NEG, F32)

    def start_exps(par):
        m_fin = [m_scr[par, hh] for hh in heads]
        for hh in heads:
            l_scr[par, hh] = jnp.zeros((1, t), F32)
        return m_fin

    def start_pv():
        for hh in heads:
            acc_scr[hh] = jnp.zeros((MLA_V, t), F32)

    def finish(par, tile):
        o_t = jnp.concatenate([acc_scr[hh] / l_scr[par, hh] for hh in heads], axis=0)
        r0 = pl.multiple_of(tile * t, t)
        gate = g_ref[0, pl.ds(r0, t), :].astype(F32)
        o_ref[0, pl.ds(r0, t), :] = (o_t.T * gate).astype(BF16)

    def run_blocks(n, fns):
        def span(base, width):
            for fn in fns:
                for u in range(width):
                    fn(base + u)

        def body(i, c):
            span(i * unroll, unroll)
            return c

        lax.fori_loop(0, n // unroll, body, 0)
        base = (n // unroll) * unroll
        width = unroll // 2
        while width >= 1:
            take = (n - base) & width
            if isinstance(take, int):
                if take:
                    span(base, width)
            else:
                pl.when(take != 0)(functools.partial(span, base, width))
            base = base + take
            width //= 2

    def step(par, qi):
        m_fin = start_exps(1 - par)
        start_scores(par)
        start_pv()
        run_blocks(jnp.maximum(qi - 1, 0),
                   [functools.partial(pv, par), functools.partial(exps, 1 - par, m_fin),
                    lambda kj: scores(par, qi, kj, False)])

        @pl.when(qi >= 1)
        def _():
            exps(1 - par, m_fin, qi - 1)
            scores(par, qi, qi - 1, False)

        scores(par, qi, qi, True)

        @pl.when(qi >= 2)
        def _():
            finish(par, qi - 2)

    def step_pair(i, carry):
        step(0, 2 * i)
        step(1, 2 * i + 1)
        return carry

    start_scores(1)
    lax.fori_loop(0, ns // 2, step_pair, 0)
    m_fin = start_exps(1)
    start_pv()
    run_blocks(ns - 1, [functools.partial(pv, 0), functools.partial(exps, 1, m_fin)])
    exps(1, m_fin, ns - 1)
    finish(0, ns - 2)
    start_pv()
    run_blocks(ns, [functools.partial(pv, 1)])
    finish(1, ns - 1)


def _mla_call(qt, k, vt, g):
    b, _, ns, _, t = qt.shape
    s = ns * t
    hp = HEADS_PER_STEP
    return pl.pallas_call(
        functools.partial(_mla_kernel, ns=ns), grid=(b, MLA_HEADS // hp),
        in_specs=[
            pl.BlockSpec((1, hp, ns, MLA_HEAD_PAD, t), lambda bi, h: (bi, h, 0, 0, 0)),
            pl.BlockSpec((1, hp, s, MLA_HEAD_PAD), lambda bi, h: (bi, h, 0, 0)),
            pl.BlockSpec((1, hp, ns, MLA_V, t), lambda bi, h: (bi, h, 0, 0, 0)),
            pl.BlockSpec((1, s, hp * MLA_V), lambda bi, h: (bi, 0, h)),
        ],
        out_specs=pl.BlockSpec((1, s, hp * MLA_V), lambda bi, h: (bi, 0, h)),
        out_shape=jax.ShapeDtypeStruct((b, s, MLA_WIDTH), BF16),
        scratch_shapes=[
            pltpu.VMEM((hp, ns, t, t), F32),
            pltpu.VMEM((hp, ns, t, t), F32),
            pltpu.VMEM((hp, ns, t, t), BF16),
            pltpu.VMEM((hp, ns, t, t), BF16),
            pltpu.VMEM((2, hp, 1, t), F32),
            pltpu.VMEM((2, hp, 1, t), F32),
            pltpu.VMEM((hp, MLA_V, t), F32),
        ],
        compiler_params=pltpu.CompilerParams(dimension_semantics=("arbitrary", "arbitrary"),
                                             vmem_limit_bytes=VMEM_LIMIT),
        name="mla_attn",
    )(qt, k, vt, g)


def _side_kernel(sinks_ref, side_ref, prev_ref, kvm_ref, o_ref):
    w = WINDOW
    tq = side_ref.shape[1]
    qi = lax.broadcasted_iota(jnp.int32, (w, 2 * w), 0)
    kj = lax.broadcasted_iota(jnp.int32, (w, 2 * w), 1)
    dist = w + qi - kj
    distf = dist.astype(F32)
    in_window = (dist >= 0) & (dist < w)
    first_mask = in_window & ((kj >= w) | (pl.program_id(1) > 0))
    lane = lax.broadcasted_iota(jnp.int32, (w, LANES), 1)
    lo = lane < SWA_HEAD_DIM
    lane_kv = lax.broadcasted_iota(jnp.int32, (2 * w, LANES), 1)
    lo_kv = lane_kv < SWA_HEAD_DIM

    for n in range(tq // w):
        rows = slice(n * w, (n + 1) * w)
        mask = first_mask if n == 0 else in_window
        for kv in range(SWA_KV_HEADS):
            kcol = slice(S_KSWA + kv * LANES, S_KSWA + (kv + 1) * LANES)
            vcol = slice(S_VSWA + kv * LANES, S_VSWA + (kv + 1) * LANES)
            if n == 0:
                k_prev = prev_ref[0, :, kv * LANES:(kv + 1) * LANES]
                v_prev = prev_ref[0, :, (SWA_KV_HEADS + kv) * LANES:(SWA_KV_HEADS + kv + 1) * LANES]
            else:
                k_prev = side_ref[0, (n - 1) * w:n * w, kcol]
                v_prev = side_ref[0, (n - 1) * w:n * w, vcol]
            kcat = jnp.concatenate([k_prev, side_ref[0, rows, kcol]], axis=0)
            vcat = jnp.concatenate([v_prev, side_ref[0, rows, vcol]], axis=0)
            v_lo = jnp.where(lo_kv, vcat, jnp.zeros_like(vcat))
            v_hi = jnp.where(lo_kv, jnp.zeros_like(vcat), vcat)
            q_parts = []
            for j in range(SWA_GROUP // 2):
                qcol = slice(S_QSWA + (kv * SWA_GROUP + 2 * j) * SWA_HEAD_DIM,
                             S_QSWA + (kv * SWA_GROUP + 2 * j + 2) * SWA_HEAD_DIM)
                qp = side_ref[0, rows, qcol]
                q_parts.append(jnp.where(lo, qp, jnp.zeros_like(qp)))
                q_parts.append(jnp.where(lo, jnp.zeros_like(qp), qp))
            qs = jnp.concatenate(q_parts, axis=0)
            sc = lax.dot_general(qs, kcat, NT_DIMS, preferred_element_type=F32)
            probs = []
            for g in range(SWA_GROUP):
                head = kv * SWA_GROUP + g
                slope = 2.0 ** (-8.0 * (head + 1) / SWA_HEADS)
                sink = sinks_ref[head]
                sg = sc[g * w:(g + 1) * w] - slope * distf
                sg = jnp.where(mask, sg, NEG)
                m = jnp.maximum(jnp.max(sg, axis=-1, keepdims=True), sink)
                e = jnp.exp(sg - m)
                den = jnp.sum(e, axis=-1, keepdims=True) + jnp.exp(sink - m)
                probs.append((e / den).astype(BF16))
            for j in range(SWA_GROUP // 2):
                out = (jnp.dot(probs[2 * j], v_lo, preferred_element_type=F32)
                       + jnp.dot(probs[2 * j + 1], v_hi, preferred_element_type=F32))
                c0 = (kv * SWA_GROUP + 2 * j) * SWA_HEAD_DIM
                gate = side_ref[0, rows, S_GSWA + c0:S_GSWA + c0 + LANES].astype(F32)
                o_ref[0, rows, c0:c0 + LANES] = (out * gate).astype(BF16)

    for h in range(MEM_HEADS):
        hc = slice(h * MEM_HEAD_DIM, (h + 1) * MEM_HEAD_DIM)
        q = side_ref[0, :, S_QMEM + h * MEM_HEAD_DIM:S_QMEM + (h + 1) * MEM_HEAD_DIM]
        sc = lax.dot_general(q, kvm_ref[0, :, hc], NT_DIMS, preferred_element_type=F32)
        m = jnp.max(sc, axis=-1, keepdims=True)
        e = jnp.exp(sc - m)
        p = (e / jnp.sum(e, axis=-1, keepdims=True)).astype(BF16)
        out = jnp.dot(p, kvm_ref[0, :, MEM_WIDTH + h * MEM_HEAD_DIM:MEM_WIDTH + (h + 1) * MEM_HEAD_DIM],
                      preferred_element_type=F32)
        gate = side_ref[0, :, S_GMEM + h * MEM_HEAD_DIM:S_GMEM + (h + 1) * MEM_HEAD_DIM].astype(F32)
        o_ref[0, :, SWA_WIDTH + h * MEM_HEAD_DIM:SWA_WIDTH + (h + 1) * MEM_HEAD_DIM] = (out * gate).astype(BF16)


def _side_call(sinks, side, kvm):
    b, s, _ = side.shape
    tq = SIDE_TILE
    per = tq // WINDOW
    m = kvm.shape[1]
    kv_cols = 2 * 2 * SWA_KV_WIDTH
    return pl.pallas_call(
        _side_kernel, grid=(b, s // tq),
        in_specs=[
            pl.BlockSpec(memory_space=pltpu.SMEM),
            pl.BlockSpec((1, tq, SIDE_COLS), lambda bi, i: (bi, i, 0)),
            pl.BlockSpec((1, WINDOW, kv_cols), lambda bi, i: (bi, jnp.maximum(i * per - 1, 0), S_KSWA // kv_cols)),
            pl.BlockSpec((1, m, 2 * MEM_WIDTH), lambda bi, i: (bi, 0, 0)),
        ],
        out_specs=pl.BlockSpec((1, tq, SWA_WIDTH + MEM_WIDTH), lambda bi, i: (bi, i, 0)),
        out_shape=jax.ShapeDtypeStruct((b, s, SWA_WIDTH + MEM_WIDTH), BF16),
        compiler_params=pltpu.CompilerParams(dimension_semantics=("arbitrary", "arbitrary"),
                                             vmem_limit_bytes=VMEM_LIMIT),
        name="side_attn",
    )(sinks, side, side, kvm)


def _out_kernel(x_ref, ya_ref, yb_ref, w_ref, fn_ref, o_ref, *, final):
    y = (x_ref[0]
         + jnp.dot(ya_ref[0], w_ref[0:MLA_WIDTH, :], preferred_element_type=F32)
         + jnp.dot(yb_ref[0], w_ref[MLA_WIDTH:, :], preferred_element_type=F32))
    if final:
        y = _rms(y, fn_ref[...])
    o_ref[0] = y


def _out_call(x, ya, yb, w, fn, final):
    b, s, _ = x.shape
    tm = OUT_TILE
    return pl.pallas_call(
        functools.partial(_out_kernel, final=final), grid=(b, s // tm),
        in_specs=[
            pl.BlockSpec((1, tm, D_MODEL), lambda bi, i: (bi, i, 0)),
            pl.BlockSpec((1, tm, MLA_WIDTH), lambda bi, i: (bi, i, 0)),
            pl.BlockSpec((1, tm, SWA_WIDTH + MEM_WIDTH), lambda bi, i: (bi, i, 0)),
            _const_spec(w.shape), _const_spec(fn.shape),
        ],
        out_specs=pl.BlockSpec((1, tm, D_MODEL), lambda bi, i: (bi, i, 0)),
        out_shape=jax.ShapeDtypeStruct((b, s, D_MODEL), F32),
        compiler_params=pltpu.CompilerParams(dimension_semantics=("arbitrary", "arbitrary"),
                                             vmem_limit_bytes=VMEM_LIMIT),
        name="out_proj",
    )(x, ya, yb, w, fn)


def _pack_w_in(w):
    c_q, c_kv, k_rope, z_mla, q_swa, k_swa, v_swa, z_swa, q_mem, z_mem = jnp.split(w, SPLIT_IDX, axis=1)
    kr = jnp.pad(k_rope, ((0, 0), (MLA_NOPE, LANES - MLA_NOPE - MLA_ROPE)))
    dup = lambda a: jnp.concatenate([a[:, :SWA_HEAD_DIM]] * 2 + [a[:, SWA_HEAD_DIM:]] * 2, axis=1)
    return jnp.concatenate([c_q, c_kv, kr, z_mla, q_swa, dup(k_swa), dup(v_swa), z_swa, q_mem, z_mem],
                           axis=1).astype(BF16)


def _pack_w_uq_t(w):
    w3 = w.reshape(MLA_Q_RANK, MLA_HEADS, MLA_NOPE + MLA_ROPE)
    w3 = jnp.pad(w3, ((0, 0), (0, 0), (0, MLA_HEAD_PAD - MLA_NOPE - MLA_ROPE)))
    return w3.reshape(MLA_Q_RANK, MLA_HEADS * MLA_HEAD_PAD).T.astype(BF16)


def _pack_w_ukv(w):
    w3 = w.reshape(MLA_KV_RANK, MLA_HEADS, MLA_NOPE + MLA_V)
    wk = jnp.pad(w3[:, :, :MLA_NOPE], ((0, 0), (0, 0), (0, MLA_HEAD_PAD - MLA_NOPE)))
    wk = wk.reshape(MLA_KV_RANK, MLA_HEADS * MLA_HEAD_PAD).astype(BF16)
    wvt = w3[:, :, MLA_NOPE:].reshape(MLA_KV_RANK, MLA_WIDTH).T.astype(BF16)
    return wk, wvt


def _rope_tables(s):
    inv = ROPE_BASE ** (-jnp.arange(0, MLA_ROPE, 2, dtype=F32) / MLA_ROPE)
    ang = jnp.arange(s, dtype=F32)[:, None] * inv[None, :]
    cos, sin = jnp.cos(ang), jnp.sin(ang)
    lead = jnp.zeros((s, MLA_NOPE), F32)
    tail = jnp.zeros((s, LANES - MLA_NOPE - MLA_ROPE), F32)
    half = jnp.zeros((s, HALF_ROPE), F32)
    cc = jnp.concatenate([lead, cos, cos, tail], axis=1)
    sa = jnp.concatenate([lead, -sin, half, tail], axis=1)
    sb = jnp.concatenate([lead, half, sin, tail], axis=1)
    return cos.T, sin.T, cc, sa, sb


def kernel(x, mem, attn_norm, w_in, mla_q_norm, w_uq, mla_kv_norm, w_ukv, swa_sinks, mem_norm, w_mem_kv, w_out,
           final_norm):
    depth = w_in.shape[0]
    s = x.shape[1]
    assert s % OUT_TILE == 0 and PROJ_TILE == ATT_TILE and s % SIDE_TILE == 0 and s % (2 * ATT_TILE) == 0
    cost, sint, cc, sa, sb = _rope_tables(s)
    fn = final_norm.reshape(1, D_MODEL)
    for l in range(depth):
        wk, wvt = _pack_w_ukv(w_ukv[l])
        qt, k, vt, gmla, side = _proj_call(
            x, attn_norm[l].reshape(1, D_MODEL), _pack_w_in(w_in[l]), mla_q_norm[l].reshape(1, MLA_Q_RANK),
            _pack_w_uq_t(w_uq[l]), mla_kv_norm[l].reshape(1, MLA_KV_RANK), wk, wvt, cost, sint, cc, sa, sb)
        kvm = _memkv_call(mem, mem_norm[l].reshape(1, D_MODEL), w_mem_kv[l].astype(BF16))
        y_mla = _mla_call(qt, k, vt, gmla)
        y_side = _side_call(swa_sinks[l], side, kvm)
        x = _out_call(x, y_mla, y_side, w_out[l].astype(BF16), fn, final=(l == depth - 1))
    return x
```

```python
import functools

import jax
import jax.numpy as jnp
import numpy as np
from jax import lax
from jax.experimental import pallas as pl
from jax.experimental.pallas import tpu as pltpu

F32 = jnp.float32
BF16 = jnp.bfloat16

D_MODEL = 1024
MLA_HEADS = 16
MLA_NOPE = 64
MLA_ROPE = 32
MLA_V = 64
MLA_Q_RANK = 384
MLA_KV_RANK = 256
ROPE_BASE = 10000.0
SWA_HEADS = 8
SWA_KV_HEADS = 2
SWA_HEAD_DIM = 64
SWA_GROUP = SWA_HEADS // SWA_KV_HEADS
WINDOW = 128
MEM_HEADS = 4
MEM_HEAD_DIM = 128
EPS = 1e-6
NEG = -1e30
LOG2_E = 1.4426950408889634

MLA_WIDTH = MLA_HEADS * MLA_V
SWA_WIDTH = SWA_HEADS * SWA_HEAD_DIM
SWA_KV_WIDTH = SWA_KV_HEADS * SWA_HEAD_DIM
MEM_WIDTH = MEM_HEADS * MEM_HEAD_DIM
SPLITS = (MLA_Q_RANK, MLA_KV_RANK, MLA_ROPE, MLA_WIDTH, SWA_WIDTH, SWA_KV_WIDTH, SWA_KV_WIDTH,
          SWA_WIDTH, MEM_WIDTH, MEM_WIDTH)
SPLIT_IDX = tuple(int(i) for i in np.cumsum(SPLITS)[:-1])

LANES = 128
MLA_HEAD_PAD = 128
HALF_ROPE = MLA_ROPE // 2
ATT_TILE = 256
PROJ_TILE = 256
SIDE_TILE = 256
OUT_TILE = 512
VMEM_LIMIT = 48 * 1024 * 1024

C_Q = 0
C_KV = C_Q + MLA_Q_RANK
C_KR = C_KV + MLA_KV_RANK
C_ZMLA = C_KR + LANES
C_QSWA = C_ZMLA + MLA_WIDTH
C_KSWA = C_QSWA + SWA_WIDTH
C_VSWA = C_KSWA + 2 * SWA_KV_WIDTH
C_ZSWA = C_VSWA + 2 * SWA_KV_WIDTH
C_QMEM = C_ZSWA + SWA_WIDTH
C_ZMEM = C_QMEM + MEM_WIDTH
W_IN_COLS = C_ZMEM + MEM_WIDTH

S_QSWA = 0
S_KSWA = S_QSWA + SWA_WIDTH
S_VSWA = S_KSWA + 2 * SWA_KV_WIDTH
S_GSWA = S_VSWA + 2 * SWA_KV_WIDTH
S_QMEM = S_GSWA + SWA_WIDTH
S_GMEM = S_QMEM + MEM_WIDTH
SIDE_COLS = S_GMEM + MEM_WIDTH

NT_DIMS = (((1,), (1,)), ((), ()))


def _rms(x, g):
    return x * lax.rsqrt(jnp.mean(x * x, axis=-1, keepdims=True) + EPS) * g


def _silu(z):
    return z * jax.nn.sigmoid(z)


def _const_spec(shape):
    nd = len(shape)
    return pl.BlockSpec(shape, lambda *_: (0,) * nd, pipeline_mode=pl.Buffered(1))


def _proj_kernel(x_ref, an_ref, win_ref, qn_ref, wuqt_ref, kvn_ref, wuk_ref, wuvt_ref,
                 cost_ref, sint_ref, cc_ref, sa_ref, sb_ref,
                 qt_ref, k_ref, vt_ref, gmla_ref, side_ref):
    hn = _rms(x_ref[0], an_ref[...]).astype(BF16)

    def proj(a, b):
        return jnp.dot(hn, win_ref[:, a:b], preferred_element_type=F32)

    cqn = _rms(proj(C_Q, C_KV), qn_ref[...]).astype(BF16)
    scale = (MLA_NOPE + MLA_ROPE) ** -0.5 * LOG2_E
    qt = lax.dot_general(wuqt_ref[...], cqn, NT_DIMS, preferred_element_type=F32) * scale
    c = cost_ref[...]
    s = sint_ref[...]
    zero_rows = jnp.zeros((MLA_HEAD_PAD - MLA_NOPE - MLA_ROPE, qt.shape[1]), BF16)
    for h in range(MLA_HEADS):
        r = h * MLA_HEAD_PAD
        x1 = qt[r + MLA_NOPE:r + MLA_NOPE + HALF_ROPE]
        x2 = qt[r + MLA_NOPE + HALF_ROPE:r + MLA_NOPE + MLA_ROPE]
        qt_ref[0, h, 0, 0:MLA_NOPE, :] = qt[r:r + MLA_NOPE].astype(BF16)
        qt_ref[0, h, 0, MLA_NOPE:MLA_NOPE + HALF_ROPE, :] = (x1 * c - x2 * s).astype(BF16)
        qt_ref[0, h, 0, MLA_NOPE + HALF_ROPE:MLA_NOPE + MLA_ROPE, :] = (x1 * s + x2 * c).astype(BF16)
        qt_ref[0, h, 0, MLA_NOPE + MLA_ROPE:MLA_HEAD_PAD, :] = zero_rows

    ckvn = _rms(proj(C_KV, C_KR), kvn_ref[...]).astype(BF16)
    kr = proj(C_KR, C_ZMLA)
    kpe = (kr * cc_ref[...] + pltpu.roll(kr, LANES - HALF_ROPE, 1) * sa_ref[...]
           + pltpu.roll(kr, HALF_ROPE, 1) * sb_ref[...])
    kk = jnp.dot(ckvn, wuk_ref[...], preferred_element_type=F32)
    for h in range(MLA_HEADS):
        k_ref[0, h, :, :] = (kk[:, h * MLA_HEAD_PAD:(h + 1) * MLA_HEAD_PAD] + kpe).astype(BF16)

    vt = lax.dot_general(wuvt_ref[...], ckvn, NT_DIMS, preferred_element_type=F32)
    for h in range(MLA_HEADS):
        vt_ref[0, h, 0, :, :] = vt[h * MLA_V:(h + 1) * MLA_V].astype(BF16)

    gmla_ref[0] = _silu(proj(C_ZMLA, C_QSWA)).astype(BF16)
    side_ref[0, :, S_QSWA:S_KSWA] = (proj(C_QSWA, C_KSWA) * (SWA_HEAD_DIM ** -0.5 * LOG2_E)).astype(BF16)
    side_ref[0, :, S_KSWA:S_GSWA] = proj(C_KSWA, C_ZSWA).astype(BF16)
    side_ref[0, :, S_GSWA:S_QMEM] = _silu(proj(C_ZSWA, C_QMEM)).astype(BF16)
    side_ref[0, :, S_QMEM:S_GMEM] = (proj(C_QMEM, C_ZMEM) * (MEM_HEAD_DIM ** -0.5 * LOG2_E)).astype(BF16)
    side_ref[0, :, S_GMEM:SIDE_COLS] = _silu(proj(C_ZMEM, W_IN_COLS)).astype(BF16)


def _proj_call(x, an, win, qn, wuqt, kvn, wuk, wuvt, cost, sint, cc, sa, sb):
    b, s, _ = x.shape
    tm = PROJ_TILE
    ns = s // tm
    in_specs = [
        pl.BlockSpec((1, tm, D_MODEL), lambda bi, i: (bi, i, 0)),
        _const_spec(an.shape), _const_spec(win.shape), _const_spec(qn.shape), _const_spec(wuqt.shape),
        _const_spec(kvn.shape), _const_spec(wuk.shape), _const_spec(wuvt.shape),
        pl.BlockSpec((HALF_ROPE, tm), lambda bi, i: (0, i)),
        pl.BlockSpec((HALF_ROPE, tm), lambda bi, i: (0, i)),
        pl.BlockSpec((tm, LANES), lambda bi, i: (i, 0)),
        pl.BlockSpec((tm, LANES), lambda bi, i: (i, 0)),
        pl.BlockSpec((tm, LANES), lambda bi, i: (i, 0)),
    ]
    out_shape = [
        jax.ShapeDtypeStruct((b, MLA_HEADS, ns, MLA_HEAD_PAD, tm), BF16),
        jax.ShapeDtypeStruct((b, MLA_HEADS, s, MLA_HEAD_PAD), BF16),
        jax.ShapeDtypeStruct((b, MLA_HEADS, ns, MLA_V, tm), BF16),
        jax.ShapeDtypeStruct((b, s, MLA_WIDTH), BF16),
        jax.ShapeDtypeStruct((b, s, SIDE_COLS), BF16),
    ]
    out_specs = [
        pl.BlockSpec((1, MLA_HEADS, 1, MLA_HEAD_PAD, tm), lambda bi, i: (bi, 0, i, 0, 0)),
        pl.BlockSpec((1, MLA_HEADS, tm, MLA_HEAD_PAD), lambda bi, i: (bi, 0, i, 0)),
        pl.BlockSpec((1, MLA_HEADS, 1, MLA_V, tm), lambda bi, i: (bi, 0, i, 0, 0)),
        pl.BlockSpec((1, tm, MLA_WIDTH), lambda bi, i: (bi, i, 0)),
        pl.BlockSpec((1, tm, SIDE_COLS), lambda bi, i: (bi, i, 0)),
    ]
    return pl.pallas_call(
        _proj_kernel, grid=(b, ns), in_specs=in_specs, out_specs=out_specs, out_shape=out_shape,
        compiler_params=pltpu.CompilerParams(dimension_semantics=("arbitrary", "arbitrary"),
                                             vmem_limit_bytes=VMEM_LIMIT),
        name="proj",
    )(x, an, win, qn, wuqt, kvn, wuk, wuvt, cost, sint, cc, sa, sb)


def _memkv_kernel(mem_ref, mn_ref, w_ref, o_ref):
    mn = _rms(mem_ref[0], mn_ref[...]).astype(BF16)
    o_ref[0] = jnp.dot(mn, w_ref[...], preferred_element_type=F32).astype(BF16)


def _memkv_call(mem, mn, w):
    b, m, _ = mem.shape
    return pl.pallas_call(
        _memkv_kernel, grid=(b,),
        in_specs=[pl.BlockSpec((1, m, D_MODEL), lambda bi: (bi, 0, 0)), _const_spec(mn.shape), _const_spec(w.shape)],
        out_specs=pl.BlockSpec((1, m, 2 * MEM_WIDTH), lambda bi: (bi, 0, 0)),
        out_shape=jax.ShapeDtypeStruct((b, m, 2 * MEM_WIDTH), BF16),
        compiler_params=pltpu.CompilerParams(dimension_semantics=("arbitrary",), vmem_limit_bytes=VMEM_LIMIT),
        name="memkv",
    )(mem, mn, w)


HEADS_PER_STEP = 2
MAX_SLABS = 4
MLA_UNROLL = 4


def _mla_kernel(qt_ref, k_ref, vt_ref, g_ref, o_ref, s0, s1, p0, p1, m_scr, l_scr, acc_scr, *, ns):
    t = ATT_TILE
    heads = range(HEADS_PER_STEP)
    unroll = MLA_UNROLL
    s_scr = (s0, s1)
    p_scr = (p0, p1)
    key_idx = lax.broadcasted_iota(jnp.int32, (t, t), 0)
    qry_idx = lax.broadcasted_iota(jnp.int32, (t, t), 1)
    causal = key_idx <= qry_idx

    def scores(par, tile, kj, masked):
        for hh in heads:
            kb = k_ref[0, hh, pl.ds(pl.multiple_of(kj * t, t), t), :]
            st = jnp.dot(kb, qt_ref[0, hh, tile], preferred_element_type=F32)
            if masked:
                st = jnp.where(causal, st, NEG)
            s_scr[par][hh, kj] = st
            slab = jnp.max(st.reshape(MAX_SLABS, t // MAX_SLABS, t), axis=0)
            m_scr[par, hh] = jnp.maximum(m_scr[par, hh], jnp.max(slab, axis=0, keepdims=True))

    def exps(par, m_fin, kj):
        for hh in heads:
            p = jnp.exp2(s_scr[par][hh, kj] - m_fin[hh])
            l_scr[par, hh] += jnp.sum(p, axis=0, keepdims=True)
            p_scr[par][hh, kj] = p.astype(BF16)

    def pv(par, kj):
        for hh in heads:
            acc_scr[hh] += jnp.dot(vt_ref[0, hh, kj], p_scr[par][hh, kj], preferred_element_type=F32)

    def start_scores(par):
        for hh in heads:
            m_scr[par, hh] = jnp.full((1, t), NEG, F32)

    def start_exps(par):
        m_fin = [m_scr[par, hh] for hh in heads]
        for hh in heads:
            l_scr[par, hh] = jnp.zeros((1, t), F32)
        return m_fin

    def start_pv():
        for hh in heads:
            acc_scr[hh] = jnp.zeros((MLA_V, t), F32)

    def finish(par, tile):
        o_t = jnp.concatenate([acc_scr[hh] / l_scr[par, hh] for hh in heads], axis=0)
        r0 = pl.multiple_of(tile * t, t)
        gate = g_ref[0, pl.ds(r0, t), :].astype(F32)
        o_ref[0, pl.ds(r0, t), :] = (o_t.T * gate).astype(BF16)

    def run_blocks(n, fns, tail):
        def span(base, width):
            for fn in fns:
                for u in range(width):
                    fn(base + u)

        def body(i, c):
            span(i * unroll, unroll)
            return c

        lax.fori_loop(0, n // unroll, body, 0)
        base = (n // unroll) * unroll
        rem = n - base

        def rest(r):
            span(base, r)
            tail()

        if isinstance(rem, int):
            rest(rem)
        else:
            for r in range(unroll):
                pl.when(rem == r)(functools.partial(rest, r))

    def step(par, qi):
        m_fin = start_exps(1 - par)
        start_scores(par)
        start_pv()

        def tail():
            exps(1 - par, m_fin, qi - 1)
            scores(par, qi, qi - 1, False)
            scores(par, qi, qi, True)
            finish(par, qi - 2)

        run_blocks(qi - 1, [functools.partial(pv, par), functools.partial(exps, 1 - par, m_fin),
                            lambda kj: scores(par, qi, kj, False)], tail)

    def step_pair(i, carry):
        step(0, 2 * i)
        step(1, 2 * i + 1)
        return carry

    start_scores(0)
    scores(0, 0, 0, True)
    m_fin = start_exps(0)
    start_scores(1)
    exps(0, m_fin, 0)
    scores(1, 1, 0, False)
    scores(1, 1, 1, True)
    lax.fori_loop(1, ns // 2, step_pair, 0)
    m_fin = start_exps(1)
    start_pv()
    run_blocks(ns - 1, [functools.partial(pv, 0), functools.partial(exps, 1, m_fin)],
               lambda: (exps(1, m_fin, ns - 1), finish(0, ns - 2)))
    start_pv()
    run_blocks(ns, [functools.partial(pv, 1)], lambda: finish(1, ns - 1))


def _mla_call(qt, k, vt, g):
    b, _, ns, _, t = qt.shape
    s = ns * t
    hp = HEADS_PER_STEP
    return pl.pallas_call(
        functools.partial(_mla_kernel, ns=ns), grid=(b, MLA_HEADS // hp),
        in_specs=[
            pl.BlockSpec((1, hp, ns, MLA_HEAD_PAD, t), lambda bi, h: (bi, h, 0, 0, 0)),
            pl.BlockSpec((1, hp, s, MLA_HEAD_PAD), lambda bi, h: (bi, h, 0, 0)),
            pl.BlockSpec((1, hp, ns, MLA_V, t), lambda bi, h: (bi, h, 0, 0, 0)),
            pl.BlockSpec((1, s, hp * MLA_V), lambda bi, h: (bi, 0, h)),
        ],
        out_specs=pl.BlockSpec((1, s, hp * MLA_V), lambda bi, h: (bi, 0, h)),
        out_shape=jax.ShapeDtypeStruct((b, s, MLA_WIDTH), BF16),
        scratch_shapes=[
            pltpu.VMEM((hp, ns, t, t), F32),
            pltpu.VMEM((hp, ns, t, t), F32),
            pltpu.VMEM((hp, ns, t, t), BF16),
            pltpu.VMEM((hp, ns, t, t), BF16),
            pltpu.VMEM((2, hp, 1, t), F32),
            pltpu.VMEM((2, hp, 1, t), F32),
            pltpu.VMEM((hp, MLA_V, t), F32),
        ],
        compiler_params=pltpu.CompilerParams(dimension_semantics=("arbitrary", "arbitrary"),
                                             vmem_limit_bytes=VMEM_LIMIT),
        name="mla_attn",
    )(qt, k, vt, g)


def _side_kernel(sinks_ref, side_ref, prev_ref, kvm_ref, o_ref, bias_scr):
    w = WINDOW
    tq = side_ref.shape[1]
    kj = lax.broadcasted_iota(jnp.int32, (w, 2 * w), 1)

    @pl.when((pl.program_id(0) == 0) & (pl.program_id(1) == 0))
    def _():
        qi = lax.broadcasted_iota(jnp.int32, (w, 2 * w), 0)
        dist = w + qi - kj
        in_window = (dist >= 0) & (dist < w)
        distf = dist.astype(F32)
        for head in range(SWA_HEADS):
            slope = 2.0 ** (-8.0 * (head + 1) / SWA_HEADS) * LOG2_E
            bias_scr[head] = jnp.where(in_window, -slope * distf, NEG)

    prev_ok = (kj >= w) | (pl.program_id(1) > 0)
    lane = lax.broadcasted_iota(jnp.int32, (w, LANES), 1)
    lo = lane < SWA_HEAD_DIM
    lane_kv = lax.broadcasted_iota(jnp.int32, (2 * w, LANES), 1)
    lo_kv = lane_kv < SWA_HEAD_DIM

    for n in range(tq // w):
        rows = slice(n * w, (n + 1) * w)
        for kv in range(SWA_KV_HEADS):
            kcol = slice(S_KSWA + kv * LANES, S_KSWA + (kv + 1) * LANES)
            vcol = slice(S_VSWA + kv * LANES, S_VSWA + (kv + 1) * LANES)
            if n == 0:
                k_prev = prev_ref[0, :, kv * LANES:(kv + 1) * LANES]
                v_prev = prev_ref[0, :, (SWA_KV_HEADS + kv) * LANES:(SWA_KV_HEADS + kv + 1) * LANES]
            else:
                k_prev = side_ref[0, (n - 1) * w:n * w, kcol]
                v_prev = side_ref[0, (n - 1) * w:n * w, vcol]
            kcat = jnp.concatenate([k_prev, side_ref[0, rows, kcol]], axis=0)
            vcat = jnp.concatenate([v_prev, side_ref[0, rows, vcol]], axis=0)
            v_lo = jnp.where(lo_kv, vcat, jnp.zeros_like(vcat))
            v_hi = jnp.where(lo_kv, jnp.zeros_like(vcat), vcat)
            q_parts = []
            for j in range(SWA_GROUP // 2):
                qcol = slice(S_QSWA + (kv * SWA_GROUP + 2 * j) * SWA_HEAD_DIM,
                             S_QSWA + (kv * SWA_GROUP + 2 * j + 2) * SWA_HEAD_DIM)
                qp = side_ref[0, rows, qcol]
                q_parts.append(jnp.where(lo, qp, jnp.zeros_like(qp)))
                q_parts.append(jnp.where(lo, jnp.zeros_like(qp), qp))
            qs = jnp.concatenate(q_parts, axis=0)
            sc = lax.dot_general(qs, kcat, NT_DIMS, preferred_element_type=F32)
            probs = []
            for g in range(SWA_GROUP):
                head = kv * SWA_GROUP + g
                sink = sinks_ref[head] * LOG2_E
                sg = sc[g * w:(g + 1) * w] + bias_scr[head]
                if n == 0:
                    sg = jnp.where(prev_ok, sg, NEG)
                m = jnp.maximum(jnp.max(sg, axis=-1, keepdims=True), sink)
                e = jnp.exp2(sg - m)
                den = jnp.sum(e, axis=-1, keepdims=True) + jnp.exp2(sink - m)
                probs.append((e * (1.0 / den)).astype(BF16))
            for j in range(SWA_GROUP // 2):
                out = (jnp.dot(probs[2 * j], v_lo, preferred_element_type=F32)
                       + jnp.dot(probs[2 * j + 1], v_hi, preferred_element_type=F32))
                c0 = (kv * SWA_GROUP + 2 * j) * SWA_HEAD_DIM
                gate = side_ref[0, rows, S_GSWA + c0:S_GSWA + c0 + LANES].astype(F32)
                o_ref[0, rows, c0:c0 + LANES] = (out * gate).astype(BF16)

    for h in range(MEM_HEADS):
        hc = slice(h * MEM_HEAD_DIM, (h + 1) * MEM_HEAD_DIM)
        q = side_ref[0, :, S_QMEM + h * MEM_HEAD_DIM:S_QMEM + (h + 1) * MEM_HEAD_DIM]
        sc = lax.dot_general(q, kvm_ref[0, :, hc], NT_DIMS, preferred_element_type=F32)
        m = jnp.max(sc, axis=-1, keepdims=True)
        e = jnp.exp2(sc - m)
        p = (e * (1.0 / jnp.sum(e, axis=-1, keepdims=True))).astype(BF16)
        out = jnp.dot(p, kvm_ref[0, :, MEM_WIDTH + h * MEM_HEAD_DIM:MEM_WIDTH + (h + 1) * MEM_HEAD_DIM],
                      preferred_element_type=F32)
        gate = side_ref[0, :, S_GMEM + h * MEM_HEAD_DIM:S_GMEM + (h + 1) * MEM_HEAD_DIM].astype(F32)
        o_ref[0, :, SWA_WIDTH + h * MEM_HEAD_DIM:SWA_WIDTH + (h + 1) * MEM_HEAD_DIM] = (out * gate).astype(BF16)


def _side_call(sinks, side, kvm):
    b, s, _ = side.shape
    tq = SIDE_TILE
    per = tq // WINDOW
    m = kvm.shape[1]
    kv_cols = 2 * 2 * SWA_KV_WIDTH
    return pl.pallas_call(
        _side_kernel, grid=(b, s // tq),
        in_specs=[
            pl.BlockSpec(memory_space=pltpu.SMEM),
            pl.BlockSpec((1, tq, SIDE_COLS), lambda bi, i: (bi, i, 0)),
            pl.BlockSpec((1, WINDOW, kv_cols), lambda bi, i: (bi, jnp.maximum(i * per - 1, 0), S_KSWA // kv_cols)),
            pl.BlockSpec((1, m, 2 * MEM_WIDTH), lambda bi, i: (bi, 0, 0)),
        ],
        out_specs=pl.BlockSpec((1, tq, SWA_WIDTH + MEM_WIDTH), lambda bi, i: (bi, i, 0)),
        out_shape=jax.ShapeDtypeStruct((b, s, SWA_WIDTH + MEM_WIDTH), BF16),
        scratch_shapes=[pltpu.VMEM((SWA_HEADS, WINDOW, 2 * WINDOW), F32)],
        compiler_params=pltpu.CompilerParams(dimension_semantics=("arbitrary", "arbitrary"),
                                             vmem_limit_bytes=VMEM_LIMIT),
        name="side_attn",
    )(sinks, side, side, kvm)


def _out_kernel(x_ref, ya_ref, yb_ref, w_ref, fn_ref, o_ref, *, final):
    y = (x_ref[0]
         + jnp.dot(ya_ref[0], w_ref[0:MLA_WIDTH, :], preferred_element_type=F32)
         + jnp.dot(yb_ref[0], w_ref[MLA_WIDTH:, :], preferred_element_type=F32))
    if final:
        y = _rms(y, fn_ref[...])
    o_ref[0] = y


def _out_call(x, ya, yb, w, fn, final):
    b, s, _ = x.shape
    tm = OUT_TILE
    return pl.pallas_call(
        functools.partial(_out_kernel, final=final), grid=(b, s // tm),
        in_specs=[
            pl.BlockSpec((1, tm, D_MODEL), lambda bi, i: (bi, i, 0)),
            pl.BlockSpec((1, tm, MLA_WIDTH), lambda bi, i: (bi, i, 0)),
            pl.BlockSpec((1, tm, SWA_WIDTH + MEM_WIDTH), lambda bi, i: (bi, i, 0)),
            _const_spec(w.shape), _const_spec(fn.shape),
        ],
        out_specs=pl.BlockSpec((1, tm, D_MODEL), lambda bi, i: (bi, i, 0)),
        out_shape=jax.ShapeDtypeStruct((b, s, D_MODEL), F32),
        compiler_params=pltpu.CompilerParams(dimension_semantics=("arbitrary", "arbitrary"),
                                             vmem_limit_bytes=VMEM_LIMIT),
        name="out_proj",
    )(x, ya, yb, w, fn)


def _pack_w_in(w):
    c_q, c_kv, k_rope, z_mla, q_swa, k_swa, v_swa, z_swa, q_mem, z_mem = jnp.split(w, SPLIT_IDX, axis=1)
    kr = jnp.pad(k_rope, ((0, 0), (MLA_NOPE, LANES - MLA_NOPE - MLA_ROPE)))
    dup = lambda a: jnp.concatenate([a[:, :SWA_HEAD_DIM]] * 2 + [a[:, SWA_HEAD_DIM:]] * 2, axis=1)
    return jnp.concatenate([c_q, c_kv, kr, z_mla, q_swa, dup(k_swa), dup(v_swa), z_swa, q_mem, z_mem],
                           axis=1).astype(BF16)


def _pack_w_uq_t(w):
    w3 = w.reshape(MLA_Q_RANK, MLA_HEADS, MLA_NOPE + MLA_ROPE)
    w3 = jnp.pad(w3, ((0, 0), (0, 0), (0, MLA_HEAD_PAD - MLA_NOPE - MLA_ROPE)))
    return w3.reshape(MLA_Q_RANK, MLA_HEADS * MLA_HEAD_PAD).T.astype(BF16)


def _pack_w_ukv(w):
    w3 = w.reshape(MLA_KV_RANK, MLA_HEADS, MLA_NOPE + MLA_V)
    wk = jnp.pad(w3[:, :, :MLA_NOPE], ((0, 0), (0, 0), (0, MLA_HEAD_PAD - MLA_NOPE)))
    wk = wk.reshape(MLA_KV_RANK, MLA_HEADS * MLA_HEAD_PAD).astype(BF16)
    wvt = w3[:, :, MLA_NOPE:].reshape(MLA_KV_RANK, MLA_WIDTH).T.astype(BF16)
    return wk, wvt


def _rope_tables(s):
    inv = ROPE_BASE ** (-jnp.arange(0, MLA_ROPE, 2, dtype=F32) / MLA_ROPE)
    ang = jnp.arange(s, dtype=F32)[:, None] * inv[None, :]
    cos, sin = jnp.cos(ang), jnp.sin(ang)
    lead = jnp.zeros((s, MLA_NOPE), F32)
    tail = jnp.zeros((s, LANES - MLA_NOPE - MLA_ROPE), F32)
    half = jnp.zeros((s, HALF_ROPE), F32)
    cc = jnp.concatenate([lead, cos, cos, tail], axis=1)
    sa = jnp.concatenate([lead, -sin, half, tail], axis=1)
    sb = jnp.concatenate([lead, half, sin, tail], axis=1)
    return cos.T, sin.T, cc, sa, sb


def kernel(x, mem, attn_norm, w_in, mla_q_norm, w_uq, mla_kv_norm, w_ukv, swa_sinks, mem_norm, w_mem_kv, w_out,
           final_norm):
    depth = w_in.shape[0]
    s = x.shape[1]
    assert s % OUT_TILE == 0 and PROJ_TILE == ATT_TILE and s % SIDE_TILE == 0 and s % (2 * ATT_TILE) == 0
    cost, sint, cc, sa, sb = _rope_tables(s)
    fn = final_norm.reshape(1, D_MODEL)
    for l in range(depth):
        wk, wvt = _pack_w_ukv(w_ukv[l])
        qt, k, vt, gmla, side = _proj_call(
            x, attn_norm[l].reshape(1, D_MODEL), _pack_w_in(w_in[l]), mla_q_norm[l].reshape(1, MLA_Q_RANK),
            _pack_w_uq_t(w_uq[l]), mla_kv_norm[l].reshape(1, MLA_KV_RANK), wk, wvt, cost, sint, cc, sa, sb)
        kvm = _memkv_call(mem, mem_norm[l].reshape(1, D_MODEL), w_mem_kv[l].astype(BF16))
        y_mla = _mla_call(qt, k, vt, gmla)
        y_side = _side_call(swa_sinks[l], side, kvm)
        x = _out_call(x, y_mla, y_side, w_out[l].astype(BF16), fn, final=(l == depth - 1))
    return x
```

```python
import functools

import jax
import jax.numpy as jnp
import numpy as np
from jax import lax
from jax.experimental import pallas as pl
from jax.experimental.pallas import tpu as pltpu

F32 = jnp.float32
BF16 = jnp.bfloat16

D_MODEL = 1024
MLA_HEADS = 16
MLA_NOPE = 64
MLA_ROPE = 32
MLA_V = 64
MLA_Q_RANK = 384
MLA_KV_RANK = 256
ROPE_BASE = 10000.0
SWA_HEADS = 8
SWA_KV_HEADS = 2
SWA_HEAD_DIM = 64
SWA_GROUP = SWA_HEADS // SWA_KV_HEADS
WINDOW = 128
MEM_HEADS = 4
MEM_HEAD_DIM = 128
EPS = 1e-6
NEG = -1e30
LOG2_E = 1.4426950408889634

MLA_WIDTH = MLA_HEADS * MLA_V
SWA_WIDTH = SWA_HEADS * SWA_HEAD_DIM
SWA_KV_WIDTH = SWA_KV_HEADS * SWA_HEAD_DIM
MEM_WIDTH = MEM_HEADS * MEM_HEAD_DIM
SPLITS = (MLA_Q_RANK, MLA_KV_RANK, MLA_ROPE, MLA_WIDTH, SWA_WIDTH, SWA_KV_WIDTH, SWA_KV_WIDTH,
          SWA_WIDTH, MEM_WIDTH, MEM_WIDTH)
SPLIT_IDX = tuple(int(i) for i in np.cumsum(SPLITS)[:-1])

LANES = 128
MLA_HEAD_PAD = 128
HALF_ROPE = MLA_ROPE // 2
ATT_TILE = 256
PROJ_TILE = 512
SIDE_TILE = 256
OUT_TILE = 512
VMEM_LIMIT = 56 * 1024 * 1024

C_Q = 0
C_KV = C_Q + MLA_Q_RANK
C_KR = C_KV + MLA_KV_RANK
C_ZMLA = C_KR + LANES
C_QSWA = C_ZMLA + MLA_WIDTH
C_KSWA = C_QSWA + SWA_WIDTH
C_VSWA = C_KSWA + 2 * SWA_KV_WIDTH
C_ZSWA = C_VSWA + 2 * SWA_KV_WIDTH
C_QMEM = C_ZSWA + SWA_WIDTH
C_ZMEM = C_QMEM + MEM_WIDTH
W_IN_COLS = C_ZMEM + MEM_WIDTH

S_QSWA = 0
S_KSWA = S_QSWA + SWA_WIDTH
S_VSWA = S_KSWA + 2 * SWA_KV_WIDTH
S_GSWA = S_VSWA + 2 * SWA_KV_WIDTH
S_QMEM = S_GSWA + SWA_WIDTH
S_GMEM = S_QMEM + MEM_WIDTH
SIDE_COLS = S_GMEM + MEM_WIDTH

NT_DIMS = (((1,), (1,)), ((), ()))


def _rms(x, g):
    return x * lax.rsqrt(jnp.mean(x * x, axis=-1, keepdims=True) + EPS) * g


def _silu(z):
    return z * jax.nn.sigmoid(z)


def _const_spec(shape):
    nd = len(shape)
    return pl.BlockSpec(shape, lambda *_: (0,) * nd, pipeline_mode=pl.Buffered(1))


def _proj_kernel(x_ref, an_ref, win_ref, qn_ref, wuqt_ref, kvn_ref, wuk_ref, wuvt_ref,
                 cost_ref, sint_ref, cc_ref, sa_ref, sb_ref,
                 qt_ref, k_ref, vt_ref, gmla_ref, side_ref):
    hn = _rms(x_ref[0], an_ref[...]).astype(BF16)

    def proj(a, b):
        return jnp.dot(hn, win_ref[:, a:b], preferred_element_type=F32)

    cqn = _rms(proj(C_Q, C_KV), qn_ref[...]).astype(BF16)
    scale = (MLA_NOPE + MLA_ROPE) ** -0.5 * LOG2_E
    qt = lax.dot_general(wuqt_ref[...], cqn, NT_DIMS, preferred_element_type=F32) * scale
    c = cost_ref[...]
    s = sint_ref[...]
    zero_rows = jnp.zeros((MLA_HEAD_PAD - MLA_NOPE - MLA_ROPE, ATT_TILE), BF16)
    att_tiles = [slice(a * ATT_TILE, (a + 1) * ATT_TILE) for a in range(qt.shape[1] // ATT_TILE)]
    for h in range(MLA_HEADS):
        r = h * MLA_HEAD_PAD
        x1 = qt[r + MLA_NOPE:r + MLA_NOPE + HALF_ROPE]
        x2 = qt[r + MLA_NOPE + HALF_ROPE:r + MLA_NOPE + MLA_ROPE]
        nope = qt[r:r + MLA_NOPE].astype(BF16)
        r1 = (x1 * c - x2 * s).astype(BF16)
        r2 = (x1 * s + x2 * c).astype(BF16)
        for a, cols in enumerate(att_tiles):
            qt_ref[0, h, a, 0:MLA_NOPE, :] = nope[:, cols]
            qt_ref[0, h, a, MLA_NOPE:MLA_NOPE + HALF_ROPE, :] = r1[:, cols]
            qt_ref[0, h, a, MLA_NOPE + HALF_ROPE:MLA_NOPE + MLA_ROPE, :] = r2[:, cols]
            qt_ref[0, h, a, MLA_NOPE + MLA_ROPE:MLA_HEAD_PAD, :] = zero_rows

    ckvn = _rms(proj(C_KV, C_KR), kvn_ref[...]).astype(BF16)
    kr = proj(C_KR, C_ZMLA)
    kpe = (kr * cc_ref[...] + pltpu.roll(kr, LANES - HALF_ROPE, 1) * sa_ref[...]
           + pltpu.roll(kr, HALF_ROPE, 1) * sb_ref[...])
    kk = jnp.dot(ckvn, wuk_ref[...], preferred_element_type=F32)
    for h in range(MLA_HEADS):
        k_ref[0, h, :, :] = (kk[:, h * MLA_HEAD_PAD:(h + 1) * MLA_HEAD_PAD] + kpe).astype(BF16)

    vt = lax.dot_general(wuvt_ref[...], ckvn, NT_DIMS, preferred_element_type=F32)
    for h in range(MLA_HEADS):
        for a, cols in enumerate(att_tiles):
            vt_ref[0, h, a, :, :] = vt[h * MLA_V:(h + 1) * MLA_V, cols].astype(BF16)

    gmla_ref[0] = _silu(proj(C_ZMLA, C_QSWA)).astype(BF16)
    side_ref[0, :, S_QSWA:S_KSWA] = (proj(C_QSWA, C_KSWA) * (SWA_HEAD_DIM ** -0.5 * LOG2_E)).astype(BF16)
    side_ref[0, :, S_KSWA:S_GSWA] = proj(C_KSWA, C_ZSWA).astype(BF16)
    side_ref[0, :, S_GSWA:S_QMEM] = _silu(proj(C_ZSWA, C_QMEM)).astype(BF16)
    side_ref[0, :, S_QMEM:S_GMEM] = (proj(C_QMEM, C_ZMEM) * (MEM_HEAD_DIM ** -0.5 * LOG2_E)).astype(BF16)
    side_ref[0, :, S_GMEM:SIDE_COLS] = _silu(proj(C_ZMEM, W_IN_COLS)).astype(BF16)


def _proj_call(x, an, win, qn, wuqt, kvn, wuk, wuvt, cost, sint, cc, sa, sb):
    b, s, _ = x.shape
    tm = PROJ_TILE
    per = tm // ATT_TILE
    ns = s // ATT_TILE
    in_specs = [
        pl.BlockSpec((1, tm, D_MODEL), lambda bi, i: (bi, i, 0)),
        _const_spec(an.shape), _const_spec(win.shape), _const_spec(qn.shape), _const_spec(wuqt.shape),
        _const_spec(kvn.shape), _const_spec(wuk.shape), _const_spec(wuvt.shape),
        pl.BlockSpec((HALF_ROPE, tm), lambda bi, i: (0, i)),
        pl.BlockSpec((HALF_ROPE, tm), lambda bi, i: (0, i)),
        pl.BlockSpec((tm, LANES), lambda bi, i: (i, 0)),
        pl.BlockSpec((tm, LANES), lambda bi, i: (i, 0)),
        pl.BlockSpec((tm, LANES), lambda bi, i: (i, 0)),
    ]
    out_shape = [
        jax.ShapeDtypeStruct((b, MLA_HEADS, ns, MLA_HEAD_PAD, ATT_TILE), BF16),
        jax.ShapeDtypeStruct((b, MLA_HEADS, s, MLA_HEAD_PAD), BF16),
        jax.ShapeDtypeStruct((b, MLA_HEADS, ns, MLA_V, ATT_TILE), BF16),
        jax.ShapeDtypeStruct((b, s, MLA_WIDTH), BF16),
        jax.ShapeDtypeStruct((b, s, SIDE_COLS), BF16),
    ]
    out_specs = [
        pl.BlockSpec((1, MLA_HEADS, per, MLA_HEAD_PAD, ATT_TILE), lambda bi, i: (bi, 0, i, 0, 0)),
        pl.BlockSpec((1, MLA_HEADS, tm, MLA_HEAD_PAD), lambda bi, i: (bi, 0, i, 0)),
        pl.BlockSpec((1, MLA_HEADS, per, MLA_V, ATT_TILE), lambda bi, i: (bi, 0, i, 0, 0)),
        pl.BlockSpec((1, tm, MLA_WIDTH), lambda bi, i: (bi, i, 0)),
        pl.BlockSpec((1, tm, SIDE_COLS), lambda bi, i: (bi, i, 0)),
    ]
    return pl.pallas_call(
        _proj_kernel, grid=(b, s // tm), in_specs=in_specs, out_specs=out_specs, out_shape=out_shape,
        compiler_params=pltpu.CompilerParams(dimension_semantics=("arbitrary", "arbitrary"),
                                             vmem_limit_bytes=VMEM_LIMIT),
        name="proj",
    )(x, an, win, qn, wuqt, kvn, wuk, wuvt, cost, sint, cc, sa, sb)


def _memkv_kernel(mem_ref, mn_ref, w_ref, o_ref):
    mn = _rms(mem_ref[0], mn_ref[...]).astype(BF16)
    o_ref[0] = jnp.dot(mn, w_ref[...], preferred_element_type=F32).astype(BF16)


def _memkv_call(mem, mn, w):
    b, m, _ = mem.shape
    return pl.pallas_call(
        _memkv_kernel, grid=(b,),
        in_specs=[pl.BlockSpec((1, m, D_MODEL), lambda bi: (bi, 0, 0)), _const_spec(mn.shape), _const_spec(w.shape)],
        out_specs=pl.BlockSpec((1, m, 2 * MEM_WIDTH), lambda bi: (bi, 0, 0)),
        out_shape=jax.ShapeDtypeStruct((b, m, 2 * MEM_WIDTH), BF16),
        compiler_params=pltpu.CompilerParams(dimension_semantics=("arbitrary",), vmem_limit_bytes=VMEM_LIMIT),
        name="memkv",
    )(mem, mn, w)


HEADS_PER_STEP = 2
MAX_SLABS = 4
TAIL_BLOCKS = 4
MLA_UNROLL = 8


def _mla_kernel(qt_ref, k_ref, vt_ref, g_ref, o_ref, s0, s1, p0, p1, m_scr, l_scr, acc_scr, *, ns):
    t = ATT_TILE
    heads = range(HEADS_PER_STEP)
    unroll = MLA_UNROLL
    s_scr = (s0, s1)
    p_scr = (p0, p1)
    key_idx = lax.broadcasted_iota(jnp.int32, (t, t), 0)
    qry_idx = lax.broadcasted_iota(jnp.int32, (t, t), 1)
    causal = key_idx <= qry_idx

    def scores(par, tile, kj, masked):
        for hh in heads:
            kb = k_ref[0, hh, pl.ds(pl.multiple_of(kj * t, t), t), :]
            st = jnp.dot(kb, qt_ref[0, hh, tile], preferred_element_type=F32)
            if masked:
                st = jnp.where(causal, st, NEG)
            s_scr[par][hh, kj] = st
            slab = jnp.max(st.reshape(MAX_SLABS, t // MAX_SLABS, t), axis=0)
            m_scr[par, hh] = jnp.maximum(m_scr[par, hh], jnp.max(slab, axis=0, keepdims=True))

    def exps(par, m_fin, kj):
        for hh in heads:
            p = jnp.exp2(s_scr[par][hh, kj] - m_fin[hh])
            l_scr[par, hh] += jnp.sum(p, axis=0, keepdims=True)
            p_scr[par][hh, kj] = p.astype(BF16)

    def pv(par, kj):
        for hh in heads:
            acc_scr[hh] += jnp.dot(vt_ref[0, hh, kj], p_scr[par][hh, kj], preferred_element_type=F32)

    def start_scores(par):
        for hh in heads:
            m_scr[par, hh] = jnp.full((1, t), NEG, F32)

    def start_exps(par):
        m_fin = [m_scr[par, hh] for hh in heads]
        for hh in heads:
            l_scr[par, hh] = jnp.zeros((1, t), F32)
        return m_fin

    def start_pv():
        for hh in heads:
            acc_scr[hh] = jnp.zeros((MLA_V, t), F32)

    def finish(par, tile):
        o_t = jnp.concatenate([acc_scr[hh] / l_scr[par, hh] for hh in heads], axis=0)
        r0 = pl.multiple_of(tile * t, t)
        gate = g_ref[0, pl.ds(r0, t), :].astype(F32)
        o_ref[0, pl.ds(r0, t), :] = (o_t.T * gate).astype(BF16)

    def run_blocks(n, fns, tail):
        def span(base, width):
            for fn in fns:
                for u in range(width):
                    fn(base + u)

        def body(i, c):
            span(i * unroll, unroll)
            return c

        lax.fori_loop(0, n // unroll, body, 0)
        base = (n // unroll) * unroll
        width = unroll // 2
        while width >= TAIL_BLOCKS:
            take = (n - base) & width
            if isinstance(take, int):
                if take:
                    span(base, width)
            else:
                pl.when(take != 0)(functools.partial(span, base, width))
            base = base + take
            width //= 2
        rem = n - base

        def rest(r):
            span(base, r)
            tail()

        if isinstance(rem, int):
            rest(rem)
        else:
            for r in range(TAIL_BLOCKS):
                pl.when(rem == r)(functools.partial(rest, r))

    def step(par, qi):
        m_fin = start_exps(1 - par)
        start_scores(par)
        start_pv()

        def tail():
            exps(1 - par, m_fin, qi - 1)
            scores(par, qi, qi - 1, False)
            scores(par, qi, qi, True)
            finish(par, qi - 2)

        run_blocks(qi - 1, [functools.partial(pv, par), functools.partial(exps, 1 - par, m_fin),
                            lambda kj: scores(par, qi, kj, False)], tail)

    def step_pair(i, carry):
        step(0, 2 * i)
        step(1, 2 * i + 1)
        return carry

    start_scores(0)
    scores(0, 0, 0, True)
    m_fin = start_exps(0)
    start_scores(1)
    exps(0, m_fin, 0)
    scores(1, 1, 0, False)
    scores(1, 1, 1, True)
    lax.fori_loop(1, ns // 2, step_pair, 0)
    m_fin = start_exps(1)
    start_pv()
    run_blocks(ns - 1, [functools.partial(pv, 0), functools.partial(exps, 1, m_fin)],
               lambda: (exps(1, m_fin, ns - 1), finish(0, ns - 2)))
    start_pv()
    run_blocks(ns, [functools.partial(pv, 1)], lambda: finish(1, ns - 1))


def _mla_call(qt, k, vt, g):
    b, _, ns, _, t = qt.shape
    s = ns * t
    hp = HEADS_PER_STEP
    return pl.pallas_call(
        functools.partial(_mla_kernel, ns=ns), grid=(b, MLA_HEADS // hp),
        in_specs=[
            pl.BlockSpec((1, hp, ns, MLA_HEAD_PAD, t), lambda bi, h: (bi, h, 0, 0, 0)),
            pl.BlockSpec((1, hp, s, MLA_HEAD_PAD), lambda bi, h: (bi, h, 0, 0)),
            pl.BlockSpec((1, hp, ns, MLA_V, t), lambda bi, h: (bi, h, 0, 0, 0)),
            pl.BlockSpec((1, s, hp * MLA_V), lambda bi, h: (bi, 0, h)),
        ],
        out_specs=pl.BlockSpec((1, s, hp * MLA_V), lambda bi, h: (bi, 0, h)),
        out_shape=jax.ShapeDtypeStruct((b, s, MLA_WIDTH), BF16),
        scratch_shapes=[
            pltpu.VMEM((hp, ns, t, t), F32),
            pltpu.VMEM((hp, ns, t, t), F32),
            pltpu.VMEM((hp, ns, t, t), BF16),
            pltpu.VMEM((hp, ns, t, t), BF16),
            pltpu.VMEM((2, hp, 1, t), F32),
            pltpu.VMEM((2, hp, 1, t), F32),
            pltpu.VMEM((hp, MLA_V, t), F32),
        ],
        compiler_params=pltpu.CompilerParams(dimension_semantics=("arbitrary", "arbitrary"),
                                             vmem_limit_bytes=VMEM_LIMIT),
        name="mla_attn",
    )(qt, k, vt, g)


def _side_kernel(sinks_ref, side_ref, prev_ref, kvm_ref, o_ref, bias_scr):
    w = WINDOW
    tq = side_ref.shape[1]
    kj = lax.broadcasted_iota(jnp.int32, (w, 2 * w), 1)

    @pl.when((pl.program_id(0) == 0) & (pl.program_id(1) == 0))
    def _():
        qi = lax.broadcasted_iota(jnp.int32, (w, 2 * w), 0)
        dist = w + qi - kj
        in_window = (dist >= 0) & (dist < w)
        distf = dist.astype(F32)
        for head in range(SWA_HEADS):
            slope = 2.0 ** (-8.0 * (head + 1) / SWA_HEADS) * LOG2_E
            bias_scr[head] = jnp.where(in_window, -slope * distf, NEG)

    prev_ok = (kj >= w) | (pl.program_id(1) > 0)
    lane = lax.broadcasted_iota(jnp.int32, (w, LANES), 1)
    lo = lane < SWA_HEAD_DIM
    lane_kv = lax.broadcasted_iota(jnp.int32, (2 * w, LANES), 1)
    lo_kv = lane_kv < SWA_HEAD_DIM

    for n in range(tq // w):
        rows = slice(n * w, (n + 1) * w)
        for kv in range(SWA_KV_HEADS):
            kcol = slice(S_KSWA + kv * LANES, S_KSWA + (kv + 1) * LANES)
            vcol = slice(S_VSWA + kv * LANES, S_VSWA + (kv + 1) * LANES)
            if n == 0:
                k_prev = prev_ref[0, :, kv * LANES:(kv + 1) * LANES]
                v_prev = prev_ref[0, :, (SWA_KV_HEADS + kv) * LANES:(SWA_KV_HEADS + kv + 1) * LANES]
            else:
                k_prev = side_ref[0, (n - 1) * w:n * w, kcol]
                v_prev = side_ref[0, (n - 1) * w:n * w, vcol]
            kcat = jnp.concatenate([k_prev, side_ref[0, rows, kcol]], axis=0)
            vcat = jnp.concatenate([v_prev, side_ref[0, rows, vcol]], axis=0)
            v_lo = jnp.where(lo_kv, vcat, jnp.zeros_like(vcat))
            v_hi = jnp.where(lo_kv, jnp.zeros_like(vcat), vcat)
            q_parts = []
            for j in range(SWA_GROUP // 2):
                qcol = slice(S_QSWA + (kv * SWA_GROUP + 2 * j) * SWA_HEAD_DIM,
                             S_QSWA + (kv * SWA_GROUP + 2 * j + 2) * SWA_HEAD_DIM)
                qp = side_ref[0, rows, qcol]
                q_parts.append(jnp.where(lo, qp, jnp.zeros_like(qp)))
                q_parts.append(jnp.where(lo, jnp.zeros_like(qp), qp))
            qs = jnp.concatenate(q_parts, axis=0)
            sc = lax.dot_general(qs, kcat, NT_DIMS, preferred_element_type=F32)
            probs = []
            for g in range(SWA_GROUP):
                head = kv * SWA_GROUP + g
                sink = sinks_ref[head] * LOG2_E
                sg = sc[g * w:(g + 1) * w] + bias_scr[head]
                if n == 0:
                    sg = jnp.where(prev_ok, sg, NEG)
                m = jnp.maximum(jnp.max(sg, axis=-1, keepdims=True), sink)
                e = jnp.exp2(sg - m)
                den = jnp.sum(e, axis=-1, keepdims=True) + jnp.exp2(sink - m)
                probs.append((e * (1.0 / den)).astype(BF16))
            for j in range(SWA_GROUP // 2):
                out = (jnp.dot(probs[2 * j], v_lo, preferred_element_type=F32)
                       + jnp.dot(probs[2 * j + 1], v_hi, preferred_element_type=F32))
                c0 = (kv * SWA_GROUP + 2 * j) * SWA_HEAD_DIM
                gate = side_ref[0, rows, S_GSWA + c0:S_GSWA + c0 + LANES].astype(F32)
                o_ref[0, rows, c0:c0 + LANES] = (out * gate).astype(BF16)

    for h in range(MEM_HEADS):
        hc = slice(h * MEM_HEAD_DIM, (h + 1) * MEM_HEAD_DIM)
        q = side_ref[0, :, S_QMEM + h * MEM_HEAD_DIM:S_QMEM + (h + 1) * MEM_HEAD_DIM]
        sc = lax.dot_general(q, kvm_ref[0, :, hc], NT_DIMS, preferred_element_type=F32)
        m = jnp.max(sc, axis=-1, keepdims=True)
        e = jnp.exp2(sc - m)
        p = (e * (1.0 / jnp.sum(e, axis=-1, keepdims=True))).astype(BF16)
        out = jnp.dot(p, kvm_ref[0, :, MEM_WIDTH + h * MEM_HEAD_DIM:MEM_WIDTH + (h + 1) * MEM_HEAD_DIM],
                      preferred_element_type=F32)
        gate = side_ref[0, :, S_GMEM + h * MEM_HEAD_DIM:S_GMEM + (h + 1) * MEM_HEAD_DIM].astype(F32)
        o_ref[0, :, SWA_WIDTH + h * MEM_HEAD_DIM:SWA_WIDTH + (h + 1) * MEM_HEAD_DIM] = (out * gate).astype(BF16)


def _side_call(sinks, side, kvm):
    b, s, _ = side.shape
    tq = SIDE_TILE
    per = tq // WINDOW
    m = kvm.shape[1]
    kv_cols = 2 * 2 * SWA_KV_WIDTH
    return pl.pallas_call(
        _side_kernel, grid=(b, s // tq),
        in_specs=[
            pl.BlockSpec(memory_space=pltpu.SMEM),
            pl.BlockSpec((1, tq, SIDE_COLS), lambda bi, i: (bi, i, 0)),
            pl.BlockSpec((1, WINDOW, kv_cols), lambda bi, i: (bi, jnp.maximum(i * per - 1, 0), S_KSWA // kv_cols)),
            pl.BlockSpec((1, m, 2 * MEM_WIDTH), lambda bi, i: (bi, 0, 0)),
        ],
        out_specs=pl.BlockSpec((1, tq, SWA_WIDTH + MEM_WIDTH), lambda bi, i: (bi, i, 0)),
        out_shape=jax.ShapeDtypeStruct((b, s, SWA_WIDTH + MEM_WIDTH), BF16),
        scratch_shapes=[pltpu.VMEM((SWA_HEADS, WINDOW, 2 * WINDOW), F32)],
        compiler_params=pltpu.CompilerParams(dimension_semantics=("arbitrary", "arbitrary"),
                                             vmem_limit_bytes=VMEM_LIMIT),
        name="side_attn",
    )(sinks, side, side, kvm)


def _out_kernel(x_ref, ya_ref, yb_ref, w_ref, fn_ref, o_ref, *, final):
    y = (x_ref[0]
         + jnp.dot(ya_ref[0], w_ref[0:MLA_WIDTH, :], preferred_element_type=F32)
         + jnp.dot(yb_ref[0], w_ref[MLA_WIDTH:, :], preferred_element_type=F32))
    if final:
        y = _rms(y, fn_ref[...])
    o_ref[0] = y


def _out_call(x, ya, yb, w, fn, final):
    b, s, _ = x.shape
    tm = OUT_TILE
    return pl.pallas_call(
        functools.partial(_out_kernel, final=final), grid=(b, s // tm),
        in_specs=[
            pl.BlockSpec((1, tm, D_MODEL), lambda bi, i: (bi, i, 0)),
            pl.BlockSpec((1, tm, MLA_WIDTH), lambda bi, i: (bi, i, 0)),
            pl.BlockSpec((1, tm, SWA_WIDTH + MEM_WIDTH), lambda bi, i: (bi, i, 0)),
            _const_spec(w.shape), _const_spec(fn.shape),
        ],
        out_specs=pl.BlockSpec((1, tm, D_MODEL), lambda bi, i: (bi, i, 0)),
        out_shape=jax.ShapeDtypeStruct((b, s, D_MODEL), F32),
        compiler_params=pltpu.CompilerParams(dimension_semantics=("arbitrary", "arbitrary"),
                                             vmem_limit_bytes=VMEM_LIMIT),
        name="out_proj",
    )(x, ya, yb, w, fn)


def _pack_w_in(w):
    c_q, c_kv, k_rope, z_mla, q_swa, k_swa, v_swa, z_swa, q_mem, z_mem = jnp.split(w, SPLIT_IDX, axis=1)
    kr = jnp.pad(k_rope, ((0, 0), (MLA_NOPE, LANES - MLA_NOPE - MLA_ROPE)))
    dup = lambda a: jnp.concatenate([a[:, :SWA_HEAD_DIM]] * 2 + [a[:, SWA_HEAD_DIM:]] * 2, axis=1)
    return jnp.concatenate([c_q, c_kv, kr, z_mla, q_swa, dup(k_swa), dup(v_swa), z_swa, q_mem, z_mem],
                           axis=1).astype(BF16)


def _pack_w_uq_t(w):
    w3 = w.reshape(MLA_Q_RANK, MLA_HEADS, MLA_NOPE + MLA_ROPE)
    w3 = jnp.pad(w3, ((0, 0), (0, 0), (0, MLA_HEAD_PAD - MLA_NOPE - MLA_ROPE)))
    return w3.reshape(MLA_Q_RANK, MLA_HEADS * MLA_HEAD_PAD).T.astype(BF16)


def _pack_w_ukv(w):
    w3 = w.reshape(MLA_KV_RANK, MLA_HEADS, MLA_NOPE + MLA_V)
    wk = jnp.pad(w3[:, :, :MLA_NOPE], ((0, 0), (0, 0), (0, MLA_HEAD_PAD - MLA_NOPE)))
    wk = wk.reshape(MLA_KV_RANK, MLA_HEADS * MLA_HEAD_PAD).astype(BF16)
    wvt = w3[:, :, MLA_NOPE:].reshape(MLA_KV_RANK, MLA_WIDTH).T.astype(BF16)
    return wk, wvt


def _rope_tables(s):
    inv = ROPE_BASE ** (-jnp.arange(0, MLA_ROPE, 2, dtype=F32) / MLA_ROPE)
    ang = jnp.arange(s, dtype=F32)[:, None] * inv[None, :]
    cos, sin = jnp.cos(ang), jnp.sin(ang)
    lead = jnp.zeros((s, MLA_NOPE), F32)
    tail = jnp.zeros((s, LANES - MLA_NOPE - MLA_ROPE), F32)
    half = jnp.zeros((s, HALF_ROPE), F32)
    cc = jnp.concatenate([lead, cos, cos, tail], axis=1)
    sa = jnp.concatenate([lead, -sin, half, tail], axis=1)
    sb = jnp.concatenate([lead, half, sin, tail], axis=1)
    return cos.T, sin.T, cc, sa, sb


def kernel(x, mem, attn_norm, w_in, mla_q_norm, w_uq, mla_kv_norm, w_ukv, swa_sinks, mem_norm, w_mem_kv, w_out,
           final_norm):
    depth = w_in.shape[0]
    s = x.shape[1]
    assert s % OUT_TILE == 0 and s % PROJ_TILE == 0 and PROJ_TILE % ATT_TILE == 0 and s % SIDE_TILE == 0 and s % (2 * ATT_TILE) == 0
    cost, sint, cc, sa, sb = _rope_tables(s)
    fn = final_norm.reshape(1, D_MODEL)
    for l in range(depth):
        wk, wvt = _pack_w_ukv(w_ukv[l])
        qt, k, vt, gmla, side = _proj_call(
            x, attn_norm[l].reshape(1, D_MODEL), _pack_w_in(w_in[l]), mla_q_norm[l].reshape(1, MLA_Q_RANK),
            _pack_w_uq_t(w_uq[l]), mla_kv_norm[l].reshape(1, MLA_KV_RANK), wk, wvt, cost, sint, cc, sa, sb)
        kvm = _memkv_call(mem, mem_norm[l].reshape(1, D_MODEL), w_mem_kv[l].astype(BF16))
        y_mla = _mla_call(qt, k, vt, gmla)
        y_side = _side_call(swa_sinks[l], side, kvm)
        x = _out_call(x, y_mla, y_side, w_out[l].astype(BF16), fn, final=(l == depth - 1))
    return x
```

```python
import functools

import jax
import jax.numpy as jnp
import numpy as np
from jax import lax
from jax.experimental import pallas as pl
from jax.experimental.pallas import tpu as pltpu

F32 = jnp.float32
BF16 = jnp.bfloat16

D_MODEL = 1024
MLA_HEADS = 16
MLA_NOPE = 64
MLA_ROPE = 32
MLA_V = 64
MLA_Q_RANK = 384
MLA_KV_RANK = 256
ROPE_BASE = 10000.0
SWA_HEADS = 8
SWA_KV_HEADS = 2
SWA_HEAD_DIM = 64
SWA_GROUP = SWA_HEADS // SWA_KV_HEADS
WINDOW = 128
MEM_HEADS = 4
MEM_HEAD_DIM = 128
EPS = 1e-6
NEG = -1e30
LOG2_E = 1.4426950408889634

MLA_WIDTH = MLA_HEADS * MLA_V
SWA_WIDTH = SWA_HEADS * SWA_HEAD_DIM
SWA_KV_WIDTH = SWA_KV_HEADS * SWA_HEAD_DIM
MEM_WIDTH = MEM_HEADS * MEM_HEAD_DIM
SPLITS = (MLA_Q_RANK, MLA_KV_RANK, MLA_ROPE, MLA_WIDTH, SWA_WIDTH, SWA_KV_WIDTH, SWA_KV_WIDTH,
          SWA_WIDTH, MEM_WIDTH, MEM_WIDTH)
SPLIT_IDX = tuple(int(i) for i in np.cumsum(SPLITS)[:-1])

LANES = 128
MLA_HEAD_PAD = 128
HALF_ROPE = MLA_ROPE // 2
ATT_TILE = 256
PROJ_TILE = 512
SIDE_TILE = 256
OUT_TILE = 512
VMEM_LIMIT = 56 * 1024 * 1024

C_Q = 0
C_KV = C_Q + MLA_Q_RANK
C_KR = C_KV + MLA_KV_RANK
C_ZMLA = C_KR + LANES
C_QSWA = C_ZMLA + MLA_WIDTH
C_KSWA = C_QSWA + SWA_WIDTH
C_VSWA = C_KSWA + SWA_KV_WIDTH
C_ZSWA = C_VSWA + SWA_KV_WIDTH
C_QMEM = C_ZSWA + SWA_WIDTH
C_ZMEM = C_QMEM + MEM_WIDTH
W_IN_COLS = C_ZMEM + MEM_WIDTH

S_QSWA = 0
S_KSWA = S_QSWA + SWA_WIDTH
S_VSWA = S_KSWA + SWA_KV_WIDTH
S_GSWA = S_VSWA + SWA_KV_WIDTH
S_QMEM = S_GSWA + SWA_WIDTH
S_GMEM = S_QMEM + MEM_WIDTH
SIDE_COLS = S_GMEM + MEM_WIDTH

NT_DIMS = (((1,), (1,)), ((), ()))


def _rms(x, g):
    return x * lax.rsqrt(jnp.mean(x * x, axis=-1, keepdims=True) + EPS) * g


def _silu(z):
    return z * jax.nn.sigmoid(z)


def _const_spec(shape):
    nd = len(shape)
    return pl.BlockSpec(shape, lambda *_: (0,) * nd, pipeline_mode=pl.Buffered(1))


def _proj_kernel(x_ref, an_ref, win_ref, qn_ref, wuqt_ref, kvn_ref, wuk_ref, wuvt_ref,
                 cost_ref, sint_ref, cc_ref, sa_ref, sb_ref,
                 qt_ref, k_ref, vt_ref, gmla_ref, side_ref):
    hn = _rms(x_ref[0], an_ref[...]).astype(BF16)

    def proj(a, b):
        return jnp.dot(hn, win_ref[:, a:b], preferred_element_type=F32)

    cqn = _rms(proj(C_Q, C_KV), qn_ref[...]).astype(BF16)
    scale = (MLA_NOPE + MLA_ROPE) ** -0.5 * LOG2_E
    qt = lax.dot_general(wuqt_ref[...], cqn, NT_DIMS, preferred_element_type=F32) * scale
    c = cost_ref[...]
    s = sint_ref[...]
    zero_rows = jnp.zeros((MLA_HEAD_PAD - MLA_NOPE - MLA_ROPE, ATT_TILE), BF16)
    att_tiles = [slice(a * ATT_TILE, (a + 1) * ATT_TILE) for a in range(qt.shape[1] // ATT_TILE)]
    for h in range(MLA_HEADS):
        r = h * (MLA_NOPE + MLA_ROPE)
        x1 = qt[r + MLA_NOPE:r + MLA_NOPE + HALF_ROPE]
        x2 = qt[r + MLA_NOPE + HALF_ROPE:r + MLA_NOPE + MLA_ROPE]
        nope = qt[r:r + MLA_NOPE].astype(BF16)
        r1 = (x1 * c - x2 * s).astype(BF16)
        r2 = (x1 * s + x2 * c).astype(BF16)
        for a, cols in enumerate(att_tiles):
            qt_ref[0, h, a, 0:MLA_NOPE, :] = nope[:, cols]
            qt_ref[0, h, a, MLA_NOPE:MLA_NOPE + HALF_ROPE, :] = r1[:, cols]
            qt_ref[0, h, a, MLA_NOPE + HALF_ROPE:MLA_NOPE + MLA_ROPE, :] = r2[:, cols]
            qt_ref[0, h, a, MLA_NOPE + MLA_ROPE:MLA_HEAD_PAD, :] = zero_rows

    ckvn = _rms(proj(C_KV, C_KR), kvn_ref[...]).astype(BF16)
    kr = proj(C_KR, C_ZMLA)
    kpe = (kr * cc_ref[...] + pltpu.roll(kr, LANES - HALF_ROPE, 1) * sa_ref[...]
           + pltpu.roll(kr, HALF_ROPE, 1) * sb_ref[...])
    kk = jnp.dot(ckvn, wuk_ref[...], preferred_element_type=F32)
    lo = lax.broadcasted_iota(jnp.int32, kpe.shape, 1) < MLA_NOPE
    for h2 in range(MLA_HEADS // 2):
        pair = kk[:, h2 * LANES:(h2 + 1) * LANES]
        k_ref[0, 2 * h2, :, :] = jnp.where(lo, pair, kpe).astype(BF16)
        k_ref[0, 2 * h2 + 1, :, :] = jnp.where(lo, pltpu.roll(pair, MLA_NOPE, 1), kpe).astype(BF16)

    vt = lax.dot_general(wuvt_ref[...], ckvn, NT_DIMS, preferred_element_type=F32)
    for h in range(MLA_HEADS):
        for a, cols in enumerate(att_tiles):
            vt_ref[0, h, a, :, :] = vt[h * MLA_V:(h + 1) * MLA_V, cols].astype(BF16)

    gmla_ref[0] = _silu(proj(C_ZMLA, C_QSWA)).astype(BF16)
    side_ref[0, :, S_QSWA:S_KSWA] = (proj(C_QSWA, C_KSWA) * (SWA_HEAD_DIM ** -0.5 * LOG2_E)).astype(BF16)
    side_ref[0, :, S_KSWA:S_GSWA] = proj(C_KSWA, C_ZSWA).astype(BF16)
    side_ref[0, :, S_GSWA:S_QMEM] = _silu(proj(C_ZSWA, C_QMEM)).astype(BF16)
    side_ref[0, :, S_QMEM:S_GMEM] = (proj(C_QMEM, C_ZMEM) * (MEM_HEAD_DIM ** -0.5 * LOG2_E)).astype(BF16)
    side_ref[0, :, S_GMEM:SIDE_COLS] = _silu(proj(C_ZMEM, W_IN_COLS)).astype(BF16)


def _proj_call(x, an, win, qn, wuqt, kvn, wuk, wuvt, cost, sint, cc, sa, sb):
    b, s, _ = x.shape
    tm = PROJ_TILE
    per = tm // ATT_TILE
    ns = s // ATT_TILE
    in_specs = [
        pl.BlockSpec((1, tm, D_MODEL), lambda bi, i: (bi, i, 0)),
        _const_spec(an.shape), _const_spec(win.shape), _const_spec(qn.shape), _const_spec(wuqt.shape),
        _const_spec(kvn.shape), _const_spec(wuk.shape), _const_spec(wuvt.shape),
        pl.BlockSpec((HALF_ROPE, tm), lambda bi, i: (0, i)),
        pl.BlockSpec((HALF_ROPE, tm), lambda bi, i: (0, i)),
        pl.BlockSpec((tm, LANES), lambda bi, i: (i, 0)),
        pl.BlockSpec((tm, LANES), lambda bi, i: (i, 0)),
        pl.BlockSpec((tm, LANES), lambda bi, i: (i, 0)),
    ]
    out_shape = [
        jax.ShapeDtypeStruct((b, MLA_HEADS, ns, MLA_HEAD_PAD, ATT_TILE), BF16),
        jax.ShapeDtypeStruct((b, MLA_HEADS, s, MLA_HEAD_PAD), BF16),
        jax.ShapeDtypeStruct((b, MLA_HEADS, ns, MLA_V, ATT_TILE), BF16),
        jax.ShapeDtypeStruct((b, s, MLA_WIDTH), BF16),
        jax.ShapeDtypeStruct((b, s, SIDE_COLS), BF16),
    ]
    out_specs = [
        pl.BlockSpec((1, MLA_HEADS, per, MLA_HEAD_PAD, ATT_TILE), lambda bi, i: (bi, 0, i, 0, 0)),
        pl.BlockSpec((1, MLA_HEADS, tm, MLA_HEAD_PAD), lambda bi, i: (bi, 0, i, 0)),
        pl.BlockSpec((1, MLA_HEADS, per, MLA_V, ATT_TILE), lambda bi, i: (bi, 0, i, 0, 0)),
        pl.BlockSpec((1, tm, MLA_WIDTH), lambda bi, i: (bi, i, 0)),
        pl.BlockSpec((1, tm, SIDE_COLS), lambda bi, i: (bi, i, 0)),
    ]
    return pl.pallas_call(
        _proj_kernel, grid=(b, s // tm), in_specs=in_specs, out_specs=out_specs, out_shape=out_shape,
        compiler_params=pltpu.CompilerParams(dimension_semantics=("arbitrary", "arbitrary"),
                                             vmem_limit_bytes=VMEM_LIMIT),
        name="proj",
    )(x, an, win, qn, wuqt, kvn, wuk, wuvt, cost, sint, cc, sa, sb)


def _memkv_kernel(mem_ref, mn_ref, w_ref, o_ref):
    mn = _rms(mem_ref[0], mn_ref[...]).astype(BF16)
    o_ref[0] = jnp.dot(mn, w_ref[...], preferred_element_type=F32).astype(BF16)


def _memkv_call(mem, mn, w):
    b, m, _ = mem.shape
    return pl.pallas_call(
        _memkv_kernel, grid=(b,),
        in_specs=[pl.BlockSpec((1, m, D_MODEL), lambda bi: (bi, 0, 0)), _const_spec(mn.shape), _const_spec(w.shape)],
        out_specs=pl.BlockSpec((1, m, 2 * MEM_WIDTH), lambda bi: (bi, 0, 0)),
        out_shape=jax.ShapeDtypeStruct((b, m, 2 * MEM_WIDTH), BF16),
        compiler_params=pltpu.CompilerParams(dimension_semantics=("arbitrary",), vmem_limit_bytes=VMEM_LIMIT),
        name="memkv",
    )(mem, mn, w)


HEADS_PER_STEP = 2
MAX_SLABS = 4
TAIL_BLOCKS = 4
MLA_UNROLL = 8


def _mla_kernel(qt_ref, k_ref, vt_ref, g_ref, o_ref, s0, s1, p0, p1, m_scr, l_scr, acc_scr, *, ns):
    t = ATT_TILE
    heads = range(HEADS_PER_STEP)
    unroll = MLA_UNROLL
    s_scr = (s0, s1)
    p_scr = (p0, p1)
    key_idx = lax.broadcasted_iota(jnp.int32, (t, t), 0)
    qry_idx = lax.broadcasted_iota(jnp.int32, (t, t), 1)
    causal = key_idx <= qry_idx

    def scores(par, tile, kj, masked):
        for hh in heads:
            kb = k_ref[0, hh, pl.ds(pl.multiple_of(kj * t, t), t), :]
            st = jnp.dot(kb, qt_ref[0, hh, tile], preferred_element_type=F32)
            if masked:
                st = jnp.where(causal, st, NEG)
            s_scr[par][hh, kj] = st
            slab = jnp.max(st.reshape(MAX_SLABS, t // MAX_SLABS, t), axis=0)
            m_scr[par, hh] = jnp.maximum(m_scr[par, hh], jnp.max(slab, axis=0, keepdims=True))

    def exps(par, m_fin, kj):
        for hh in heads:
            p = jnp.exp2(s_scr[par][hh, kj] - m_fin[hh])
            l_scr[par, hh] += jnp.sum(p, axis=0, keepdims=True)
            p_scr[par][hh, kj] = p.astype(BF16)

    def pv(par, kj):
        for hh in heads:
            acc_scr[hh] += jnp.dot(vt_ref[0, hh, kj], p_scr[par][hh, kj], preferred_element_type=F32)

    def start_scores(par):
        for hh in heads:
            m_scr[par, hh] = jnp.full((1, t), NEG, F32)

    def start_exps(par):
        m_fin = [m_scr[par, hh] for hh in heads]
        for hh in heads:
            l_scr[par, hh] = jnp.zeros((1, t), F32)
        return m_fin

    def start_pv():
        for hh in heads:
            acc_scr[hh] = jnp.zeros((MLA_V, t), F32)

    def finish(par, tile):
        o_t = jnp.concatenate([acc_scr[hh] / l_scr[par, hh] for hh in heads], axis=0)
        r0 = pl.multiple_of(tile * t, t)
        gate = g_ref[0, pl.ds(r0, t), :].astype(F32)
        o_ref[0, pl.ds(r0, t), :] = (o_t.T * gate).astype(BF16)

    def run_blocks(n, fns, tail):
        def span(base, width):
            for fn in fns:
                for u in range(width):
                    fn(base + u)

        def body(i, c):
            span(i * unroll, unroll)
            return c

        lax.fori_loop(0, n // unroll, body, 0)
        base = (n // unroll) * unroll
        width = unroll // 2
        while width >= TAIL_BLOCKS:
            take = (n - base) & width
            if isinstance(take, int):
                if take:
                    span(base, width)
            else:
                pl.when(take != 0)(functools.partial(span, base, width))
            base = base + take
            width //= 2
        rem = n - base

        def rest(r):
            span(base, r)
            tail()

        if isinstance(rem, int):
            rest(rem)
        else:
            for r in range(TAIL_BLOCKS):
                pl.when(rem == r)(functools.partial(rest, r))

    def step(par, qi):
        m_fin = start_exps(1 - par)
        start_scores(par)
        start_pv()

        def tail():
            exps(1 - par, m_fin, qi - 1)
            scores(par, qi, qi - 1, False)
            scores(par, qi, qi, True)
            finish(par, qi - 2)

        run_blocks(qi - 1, [functools.partial(pv, par), functools.partial(exps, 1 - par, m_fin),
                            lambda kj: scores(par, qi, kj, False)], tail)

    def step_pair(i, carry):
        step(0, 2 * i)
        step(1, 2 * i + 1)
        return carry

    start_scores(0)
    scores(0, 0, 0, True)
    m_fin = start_exps(0)
    start_scores(1)
    exps(0, m_fin, 0)
    scores(1, 1, 0, False)
    scores(1, 1, 1, True)
    lax.fori_loop(1, ns // 2, step_pair, 0)
    m_fin = start_exps(1)
    start_pv()
    run_blocks(ns - 1, [functools.partial(pv, 0), functools.partial(exps, 1, m_fin)],
               lambda: (exps(1, m_fin, ns - 1), finish(0, ns - 2)))
    start_pv()
    run_blocks(ns, [functools.partial(pv, 1)], lambda: finish(1, ns - 1))


def _mla_call(qt, k, vt, g):
    b, _, ns, _, t = qt.shape
    s = ns * t
    hp = HEADS_PER_STEP
    return pl.pallas_call(
        functools.partial(_mla_kernel, ns=ns), grid=(b, MLA_HEADS // hp),
        in_specs=[
            pl.BlockSpec((1, hp, ns, MLA_HEAD_PAD, t), lambda bi, h: (bi, h, 0, 0, 0)),
            pl.BlockSpec((1, hp, s, MLA_HEAD_PAD), lambda bi, h: (bi, h, 0, 0)),
            pl.BlockSpec((1, hp, ns, MLA_V, t), lambda bi, h: (bi, h, 0, 0, 0)),
            pl.BlockSpec((1, s, hp * MLA_V), lambda bi, h: (bi, 0, h)),
        ],
        out_specs=pl.BlockSpec((1, s, hp * MLA_V), lambda bi, h: (bi, 0, h)),
        out_shape=jax.ShapeDtypeStruct((b, s, MLA_WIDTH), BF16),
        scratch_shapes=[
            pltpu.VMEM((hp, ns, t, t), F32),
            pltpu.VMEM((hp, ns, t, t), F32),
            pltpu.VMEM((hp, ns, t, t), BF16),
            pltpu.VMEM((hp, ns, t, t), BF16),
            pltpu.VMEM((2, hp, 1, t), F32),
            pltpu.VMEM((2, hp, 1, t), F32),
            pltpu.VMEM((hp, MLA_V, t), F32),
        ],
        compiler_params=pltpu.CompilerParams(dimension_semantics=("arbitrary", "arbitrary"),
                                             vmem_limit_bytes=VMEM_LIMIT),
        name="mla_attn",
    )(qt, k, vt, g)


def _side_kernel(sinks_ref, side_ref, prev_ref, kvm_ref, o_ref, bias_scr):
    w = WINDOW
    tq = side_ref.shape[1]
    kj = lax.broadcasted_iota(jnp.int32, (w, 2 * w), 1)

    @pl.when((pl.program_id(0) == 0) & (pl.program_id(1) == 0))
    def _():
        qi = lax.broadcasted_iota(jnp.int32, (w, 2 * w), 0)
        dist = w + qi - kj
        in_window = (dist >= 0) & (dist < w)
        distf = dist.astype(F32)
        for head in range(SWA_HEADS):
            slope = 2.0 ** (-8.0 * (head + 1) / SWA_HEADS) * LOG2_E
            bias_scr[head] = jnp.where(in_window, -slope * distf, NEG)

    prev_ok = (kj >= w) | (pl.program_id(1) > 0)
    lane = lax.broadcasted_iota(jnp.int32, (w, LANES), 1)
    lo = lane < SWA_HEAD_DIM
    lane_kv = lax.broadcasted_iota(jnp.int32, (2 * w, LANES), 1)
    lo_kv = lane_kv < SWA_HEAD_DIM

    for n in range(tq // w):
        rows = slice(n * w, (n + 1) * w)
        kcol = slice(S_KSWA, S_KSWA + LANES)
        vcol = slice(S_VSWA, S_VSWA + LANES)
        if n == 0:
            k_prev = prev_ref[0, :, 0:LANES]
            v_prev = prev_ref[0, :, LANES:2 * LANES]
        else:
            k_prev = side_ref[0, (n - 1) * w:n * w, kcol]
            v_prev = side_ref[0, (n - 1) * w:n * w, vcol]
        k_both = jnp.concatenate([k_prev, side_ref[0, rows, kcol]], axis=0)
        v_both = jnp.concatenate([v_prev, side_ref[0, rows, vcol]], axis=0)
        k_swap = pltpu.roll(k_both, SWA_HEAD_DIM, 1)
        v_swap = pltpu.roll(v_both, SWA_HEAD_DIM, 1)
        zero_kv = jnp.zeros_like(v_both)
        for kv in range(SWA_KV_HEADS):
            own_lo, own_hi = (k_both, k_swap) if kv == 0 else (k_swap, k_both)
            kcat = jnp.where(lo_kv, own_lo, own_hi)
            v_lo = jnp.where(lo_kv, v_both if kv == 0 else v_swap, zero_kv)
            v_hi = jnp.where(lo_kv, zero_kv, v_swap if kv == 0 else v_both)
            q_parts = []
            for j in range(SWA_GROUP // 2):
                qcol = slice(S_QSWA + (kv * SWA_GROUP + 2 * j) * SWA_HEAD_DIM,
                             S_QSWA + (kv * SWA_GROUP + 2 * j + 2) * SWA_HEAD_DIM)
                qp = side_ref[0, rows, qcol]
                q_parts.append(jnp.where(lo, qp, jnp.zeros_like(qp)))
                q_parts.append(jnp.where(lo, jnp.zeros_like(qp), qp))
            qs = jnp.concatenate(q_parts, axis=0)
            sc = lax.dot_general(qs, kcat, NT_DIMS, preferred_element_type=F32)
            probs = []
            for g in range(SWA_GROUP):
                head = kv * SWA_GROUP + g
                sink = sinks_ref[head] * LOG2_E
                sg = sc[g * w:(g + 1) * w] + bias_scr[head]
                if n == 0:
                    sg = jnp.where(prev_ok, sg, NEG)
                m = jnp.maximum(jnp.max(sg, axis=-1, keepdims=True), sink)
                e = jnp.exp2(sg - m)
                den = jnp.sum(e, axis=-1, keepdims=True) + jnp.exp2(sink - m)
                probs.append((e * (1.0 / den)).astype(BF16))
            for j in range(SWA_GROUP // 2):
                out = (jnp.dot(probs[2 * j], v_lo, preferred_element_type=F32)
                       + jnp.dot(probs[2 * j + 1], v_hi, preferred_element_type=F32))
                c0 = (kv * SWA_GROUP + 2 * j) * SWA_HEAD_DIM
                gate = side_ref[0, rows, S_GSWA + c0:S_GSWA + c0 + LANES].astype(F32)
                o_ref[0, rows, c0:c0 + LANES] = (out * gate).astype(BF16)

    for h in range(MEM_HEADS):
        hc = slice(h * MEM_HEAD_DIM, (h + 1) * MEM_HEAD_DIM)
        q = side_ref[0, :, S_QMEM + h * MEM_HEAD_DIM:S_QMEM + (h + 1) * MEM_HEAD_DIM]
        sc = lax.dot_general(q, kvm_ref[0, :, hc], NT_DIMS, preferred_element_type=F32)
        m = jnp.max(sc, axis=-1, keepdims=True)
        e = jnp.exp2(sc - m)
        p = (e * (1.0 / jnp.sum(e, axis=-1, keepdims=True))).astype(BF16)
        out = jnp.dot(p, kvm_ref[0, :, MEM_WIDTH + h * MEM_HEAD_DIM:MEM_WIDTH + (h + 1) * MEM_HEAD_DIM],
                      preferred_element_type=F32)
        gate = side_ref[0, :, S_GMEM + h * MEM_HEAD_DIM:S_GMEM + (h + 1) * MEM_HEAD_DIM].astype(F32)
        o_ref[0, :, SWA_WIDTH + h * MEM_HEAD_DIM:SWA_WIDTH + (h + 1) * MEM_HEAD_DIM] = (out * gate).astype(BF16)


def _side_call(sinks, side, kvm):
    b, s, _ = side.shape
    tq = SIDE_TILE
    per = tq // WINDOW
    m = kvm.shape[1]
    kv_cols = 2 * SWA_KV_WIDTH
    return pl.pallas_call(
        _side_kernel, grid=(b, s // tq),
        in_specs=[
            pl.BlockSpec(memory_space=pltpu.SMEM),
            pl.BlockSpec((1, tq, SIDE_COLS), lambda bi, i: (bi, i, 0)),
            pl.BlockSpec((1, WINDOW, kv_cols), lambda bi, i: (bi, jnp.maximum(i * per - 1, 0), S_KSWA // kv_cols)),
            pl.BlockSpec((1, m, 2 * MEM_WIDTH), lambda bi, i: (bi, 0, 0)),
        ],
        out_specs=pl.BlockSpec((1, tq, SWA_WIDTH + MEM_WIDTH), lambda bi, i: (bi, i, 0)),
        out_shape=jax.ShapeDtypeStruct((b, s, SWA_WIDTH + MEM_WIDTH), BF16),
        scratch_shapes=[pltpu.VMEM((SWA_HEADS, WINDOW, 2 * WINDOW), F32)],
        compiler_params=pltpu.CompilerParams(dimension_semantics=("arbitrary", "arbitrary"),
                                             vmem_limit_bytes=VMEM_LIMIT),
        name="side_attn",
    )(sinks, side, side, kvm)


def _out_kernel(x_ref, ya_ref, yb_ref, w_ref, fn_ref, o_ref, *, final):
    y = (x_ref[0]
         + jnp.dot(ya_ref[0], w_ref[0:MLA_WIDTH, :], preferred_element_type=F32)
         + jnp.dot(yb_ref[0], w_ref[MLA_WIDTH:, :], preferred_element_type=F32))
    if final:
        y = _rms(y, fn_ref[...])
    o_ref[0] = y


def _out_call(x, ya, yb, w, fn, final):
    b, s, _ = x.shape
    tm = OUT_TILE
    return pl.pallas_call(
        functools.partial(_out_kernel, final=final), grid=(b, s // tm),
        in_specs=[
            pl.BlockSpec((1, tm, D_MODEL), lambda bi, i: (bi, i, 0)),
            pl.BlockSpec((1, tm, MLA_WIDTH), lambda bi, i: (bi, i, 0)),
            pl.BlockSpec((1, tm, SWA_WIDTH + MEM_WIDTH), lambda bi, i: (bi, i, 0)),
            _const_spec(w.shape), _const_spec(fn.shape),
        ],
        out_specs=pl.BlockSpec((1, tm, D_MODEL), lambda bi, i: (bi, i, 0)),
        out_shape=jax.ShapeDtypeStruct((b, s, D_MODEL), F32),
        compiler_params=pltpu.CompilerParams(dimension_semantics=("arbitrary", "arbitrary"),
                                             vmem_limit_bytes=VMEM_LIMIT),
        name="out_proj",
    )(x, ya, yb, w, fn)


def _pack_w_in(w):
    c_q, c_kv, k_rope, z_mla, q_swa, k_swa, v_swa, z_swa, q_mem, z_mem = jnp.split(w, SPLIT_IDX, axis=1)
    kr = jnp.pad(k_rope, ((0, 0), (MLA_NOPE, LANES - MLA_NOPE - MLA_ROPE)))
    return jnp.concatenate([c_q, c_kv, kr, z_mla, q_swa, k_swa, v_swa, z_swa, q_mem, z_mem],
                           axis=1).astype(BF16)


def _pack_w_uq_t(w):
    return w.T.astype(BF16)


def _pack_w_ukv(w):
    w3 = w.reshape(MLA_KV_RANK, MLA_HEADS, MLA_NOPE + MLA_V)
    wk = w3[:, :, :MLA_NOPE].reshape(MLA_KV_RANK, MLA_HEADS * MLA_NOPE).astype(BF16)
    wvt = w3[:, :, MLA_NOPE:].reshape(MLA_KV_RANK, MLA_WIDTH).T.astype(BF16)
    return wk, wvt


def _rope_tables(s):
    inv = ROPE_BASE ** (-jnp.arange(0, MLA_ROPE, 2, dtype=F32) / MLA_ROPE)
    ang = jnp.arange(s, dtype=F32)[:, None] * inv[None, :]
    cos, sin = jnp.cos(ang), jnp.sin(ang)
    lead = jnp.zeros((s, MLA_NOPE), F32)
    tail = jnp.zeros((s, LANES - MLA_NOPE - MLA_ROPE), F32)
    half = jnp.zeros((s, HALF_ROPE), F32)
    cc = jnp.concatenate([lead, cos, cos, tail], axis=1)
    sa = jnp.concatenate([lead, -sin, half, tail], axis=1)
    sb = jnp.concatenate([lead, half, sin, tail], axis=1)
    return cos.T, sin.T, cc, sa, sb


def kernel(x, mem, attn_norm, w_in, mla_q_norm, w_uq, mla_kv_norm, w_ukv, swa_sinks, mem_norm, w_mem_kv, w_out,
           final_norm):
    depth = w_in.shape[0]
    s = x.shape[1]
    assert s % OUT_TILE == 0 and s % PROJ_TILE == 0 and PROJ_TILE % ATT_TILE == 0 and s % SIDE_TILE == 0 and s % (2 * ATT_TILE) == 0
    cost, sint, cc, sa, sb = _rope_tables(s)
    fn = final_norm.reshape(1, D_MODEL)
    for l in range(depth):
        wk, wvt = _pack_w_ukv(w_ukv[l])
        qt, k, vt, gmla, side = _proj_call(
            x, attn_norm[l].reshape(1, D_MODEL), _pack_w_in(w_in[l]), mla_q_norm[l].reshape(1, MLA_Q_RANK),
            _pack_w_uq_t(w_uq[l]), mla_kv_norm[l].reshape(1, MLA_KV_RANK), wk, wvt, cost, sint, cc, sa, sb)
        kvm = _memkv_call(mem, mem_norm[l].reshape(1, D_MODEL), w_mem_kv[l].astype(BF16))
        y_mla = _mla_call(qt, k, vt, gmla)
        y_side = _side_call(swa_sinks[l], side, kvm)
        x = _out_call(x, y_mla, y_side, w_out[l].astype(BF16), fn, final=(l == depth - 1))
    return x
```

```python
import functools

import jax
import jax.numpy as jnp
import numpy as np
from jax import lax
from jax.experimental import pallas as pl
from jax.experimental.pallas import tpu as pltpu

F32 = jnp.float32
BF16 = jnp.bfloat16

D_MODEL = 1024
MLA_HEADS = 16
MLA_NOPE = 64
MLA_ROPE = 32
MLA_V = 64
MLA_Q_RANK = 384
MLA_KV_RANK = 256
ROPE_BASE = 10000.0
SWA_HEADS = 8
SWA_KV_HEADS = 2
SWA_HEAD_DIM = 64
SWA_GROUP = SWA_HEADS // SWA_KV_HEADS
WINDOW = 128
MEM_HEADS = 4
MEM_HEAD_DIM = 128
EPS = 1e-6
NEG = -1e30
LOG2_E = 1.4426950408889634

MLA_WIDTH = MLA_HEADS * MLA_V
SWA_WIDTH = SWA_HEADS * SWA_HEAD_DIM
SWA_KV_WIDTH = SWA_KV_HEADS * SWA_HEAD_DIM
MEM_WIDTH = MEM_HEADS * MEM_HEAD_DIM
SPLITS = (MLA_Q_RANK, MLA_KV_RANK, MLA_ROPE, MLA_WIDTH, SWA_WIDTH, SWA_KV_WIDTH, SWA_KV_WIDTH,
          SWA_WIDTH, MEM_WIDTH, MEM_WIDTH)
SPLIT_IDX = tuple(int(i) for i in np.cumsum(SPLITS)[:-1])

LANES = 128
MLA_HEAD_PAD = 128
HALF_ROPE = MLA_ROPE // 2
ATT_TILE = 256
PROJ_TILE = 512
SIDE_TILE = 512
VMEM_LIMIT = 56 * 1024 * 1024

C_Q = 0
C_KV = C_Q + MLA_Q_RANK
C_KR = C_KV + MLA_KV_RANK
C_ZMLA = C_KR + LANES
C_QSWA = C_ZMLA + MLA_WIDTH
C_KSWA = C_QSWA + SWA_WIDTH
C_VSWA = C_KSWA + SWA_KV_WIDTH
C_ZSWA = C_VSWA + SWA_KV_WIDTH
C_QMEM = C_ZSWA + SWA_WIDTH
C_ZMEM = C_QMEM + MEM_WIDTH
W_IN_COLS = C_ZMEM + MEM_WIDTH

S_QSWA = 0
S_KSWA = S_QSWA + SWA_WIDTH
S_VSWA = S_KSWA + SWA_KV_WIDTH
S_GSWA = S_VSWA + SWA_KV_WIDTH
S_QMEM = S_GSWA + SWA_WIDTH
S_GMEM = S_QMEM + MEM_WIDTH
SIDE_COLS = S_GMEM + MEM_WIDTH

NT_DIMS = (((1,), (1,)), ((), ()))


def _rms(x, g):
    return x * lax.rsqrt(jnp.mean(x * x, axis=-1, keepdims=True) + EPS) * g


def _silu(z):
    return z * jax.nn.sigmoid(z)


def _const_spec(shape):
    nd = len(shape)
    return pl.BlockSpec(shape, lambda *_: (0,) * nd, pipeline_mode=pl.Buffered(1))


def _proj_kernel(x_ref, an_ref, win_ref, qn_ref, wuqt_ref, kvn_ref, wuk_ref, wuvt_ref,
                 cost_ref, sint_ref, cc_ref, sa_ref, sb_ref,
                 qt_ref, k_ref, vt_ref, gmla_ref, side_ref):
    hn = _rms(x_ref[0], an_ref[...]).astype(BF16)

    def proj(a, b):
        return jnp.dot(hn, win_ref[:, a:b], preferred_element_type=F32)

    cqn = _rms(proj(C_Q, C_KV), qn_ref[...]).astype(BF16)
    scale = (MLA_NOPE + MLA_ROPE) ** -0.5 * LOG2_E
    qt = lax.dot_general(wuqt_ref[...], cqn, NT_DIMS, preferred_element_type=F32) * scale
    c = cost_ref[...]
    s = sint_ref[...]
    zero_rows = jnp.zeros((MLA_HEAD_PAD - MLA_NOPE - MLA_ROPE, ATT_TILE), BF16)
    att_tiles = [slice(a * ATT_TILE, (a + 1) * ATT_TILE) for a in range(qt.shape[1] // ATT_TILE)]
    for h in range(MLA_HEADS):
        r = h * (MLA_NOPE + MLA_ROPE)
        x1 = qt[r + MLA_NOPE:r + MLA_NOPE + HALF_ROPE]
        x2 = qt[r + MLA_NOPE + HALF_ROPE:r + MLA_NOPE + MLA_ROPE]
        nope = qt[r:r + MLA_NOPE].astype(BF16)
        r1 = (x1 * c - x2 * s).astype(BF16)
        r2 = (x1 * s + x2 * c).astype(BF16)
        for a, cols in enumerate(att_tiles):
            qt_ref[0, h, a, 0:MLA_NOPE, :] = nope[:, cols]
            qt_ref[0, h, a, MLA_NOPE:MLA_NOPE + HALF_ROPE, :] = r1[:, cols]
            qt_ref[0, h, a, MLA_NOPE + HALF_ROPE:MLA_NOPE + MLA_ROPE, :] = r2[:, cols]
            qt_ref[0, h, a, MLA_NOPE + MLA_ROPE:MLA_HEAD_PAD, :] = zero_rows

    ckvn = _rms(proj(C_KV, C_KR), kvn_ref[...]).astype(BF16)
    kr = proj(C_KR, C_ZMLA)
    kpe = (kr * cc_ref[...] + pltpu.roll(kr, LANES - HALF_ROPE, 1) * sa_ref[...]
           + pltpu.roll(kr, HALF_ROPE, 1) * sb_ref[...])
    kk = jnp.dot(ckvn, wuk_ref[...], preferred_element_type=F32)
    lo = lax.broadcasted_iota(jnp.int32, kpe.shape, 1) < MLA_NOPE
    for h2 in range(MLA_HEADS // 2):
        pair = kk[:, h2 * LANES:(h2 + 1) * LANES]
        k_ref[0, 2 * h2, :, :] = jnp.where(lo, pair, kpe).astype(BF16)
        k_ref[0, 2 * h2 + 1, :, :] = jnp.where(lo, pltpu.roll(pair, MLA_NOPE, 1), kpe).astype(BF16)

    vt = lax.dot_general(wuvt_ref[...], ckvn, NT_DIMS, preferred_element_type=F32)
    for h in range(MLA_HEADS):
        for a, cols in enumerate(att_tiles):
            vt_ref[0, h, a, :, :] = vt[h * MLA_V:(h + 1) * MLA_V, cols].astype(BF16)

    gmla_ref[0] = _silu(proj(C_ZMLA, C_QSWA)).astype(BF16)
    side_ref[0, :, S_QSWA:S_KSWA] = (proj(C_QSWA, C_KSWA) * (SWA_HEAD_DIM ** -0.5 * LOG2_E)).astype(BF16)
    side_ref[0, :, S_KSWA:S_GSWA] = proj(C_KSWA, C_ZSWA).astype(BF16)
    side_ref[0, :, S_GSWA:S_QMEM] = _silu(proj(C_ZSWA, C_QMEM)).astype(BF16)
    side_ref[0, :, S_QMEM:S_GMEM] = (proj(C_QMEM, C_ZMEM) * (MEM_HEAD_DIM ** -0.5 * LOG2_E)).astype(BF16)
    side_ref[0, :, S_GMEM:SIDE_COLS] = _silu(proj(C_ZMEM, W_IN_COLS)).astype(BF16)


def _proj_call(x, an, win, qn, wuqt, kvn, wuk, wuvt, cost, sint, cc, sa, sb):
    b, s, _ = x.shape
    tm = PROJ_TILE
    per = tm // ATT_TILE
    ns = s // ATT_TILE
    in_specs = [
        pl.BlockSpec((1, tm, D_MODEL), lambda bi, i: (bi, i, 0)),
        _const_spec(an.shape), _const_spec(win.shape), _const_spec(qn.shape), _const_spec(wuqt.shape),
        _const_spec(kvn.shape), _const_spec(wuk.shape), _const_spec(wuvt.shape),
        pl.BlockSpec((HALF_ROPE, tm), lambda bi, i: (0, i)),
        pl.BlockSpec((HALF_ROPE, tm), lambda bi, i: (0, i)),
        pl.BlockSpec((tm, LANES), lambda bi, i: (i, 0)),
        pl.BlockSpec((tm, LANES), lambda bi, i: (i, 0)),
        pl.BlockSpec((tm, LANES), lambda bi, i: (i, 0)),
    ]
    out_shape = [
        jax.ShapeDtypeStruct((b, MLA_HEADS, ns, MLA_HEAD_PAD, ATT_TILE), BF16),
        jax.ShapeDtypeStruct((b, MLA_HEADS, s, MLA_HEAD_PAD), BF16),
        jax.ShapeDtypeStruct((b, MLA_HEADS, ns, MLA_V, ATT_TILE), BF16),
        jax.ShapeDtypeStruct((b, s, MLA_WIDTH), BF16),
        jax.ShapeDtypeStruct((b, s, SIDE_COLS), BF16),
    ]
    out_specs = [
        pl.BlockSpec((1, MLA_HEADS, per, MLA_HEAD_PAD, ATT_TILE), lambda bi, i: (bi, 0, i, 0, 0)),
        pl.BlockSpec((1, MLA_HEADS, tm, MLA_HEAD_PAD), lambda bi, i: (bi, 0, i, 0)),
        pl.BlockSpec((1, MLA_HEADS, per, MLA_V, ATT_TILE), lambda bi, i: (bi, 0, i, 0, 0)),
        pl.BlockSpec((1, tm, MLA_WIDTH), lambda bi, i: (bi, i, 0)),
        pl.BlockSpec((1, tm, SIDE_COLS), lambda bi, i: (bi, i, 0)),
    ]
    return pl.pallas_call(
        _proj_kernel, grid=(b, s // tm), in_specs=in_specs, out_specs=out_specs, out_shape=out_shape,
        compiler_params=pltpu.CompilerParams(dimension_semantics=("arbitrary", "arbitrary"),
                                             vmem_limit_bytes=VMEM_LIMIT),
        name="proj",
    )(x, an, win, qn, wuqt, kvn, wuk, wuvt, cost, sint, cc, sa, sb)


def _memkv_kernel(mem_ref, mn_ref, w_ref, o_ref):
    mn = _rms(mem_ref[0], mn_ref[...]).astype(BF16)
    o_ref[0] = jnp.dot(mn, w_ref[...], preferred_element_type=F32).astype(BF16)


def _memkv_call(mem, mn, w):
    b, m, _ = mem.shape
    return pl.pallas_call(
        _memkv_kernel, grid=(b,),
        in_specs=[pl.BlockSpec((1, m, D_MODEL), lambda bi: (bi, 0, 0)), _const_spec(mn.shape), _const_spec(w.shape)],
        out_specs=pl.BlockSpec((1, m, 2 * MEM_WIDTH), lambda bi: (bi, 0, 0)),
        out_shape=jax.ShapeDtypeStruct((b, m, 2 * MEM_WIDTH), BF16),
        compiler_params=pltpu.CompilerParams(dimension_semantics=("arbitrary",), vmem_limit_bytes=VMEM_LIMIT),
        name="memkv",
    )(mem, mn, w)


HEADS_PER_STEP = 2
MAX_SLABS = 4
TAIL_BLOCKS = 4
MLA_UNROLL = 8


def _mla_kernel(qt_ref, k_ref, vt_ref, g_ref, o_ref, s0, s1, p0, p1, m_scr, l_scr, acc_scr, *, ns):
    t = ATT_TILE
    heads = range(HEADS_PER_STEP)
    unroll = MLA_UNROLL
    s_scr = (s0, s1)
    p_scr = (p0, p1)
    key_idx = lax.broadcasted_iota(jnp.int32, (t, t), 0)
    qry_idx = lax.broadcasted_iota(jnp.int32, (t, t), 1)
    causal = key_idx <= qry_idx

    def scores(par, tile, kj, masked):
        for hh in heads:
            kb = k_ref[0, hh, pl.ds(pl.multiple_of(kj * t, t), t), :]
            st = jnp.dot(kb, qt_ref[0, hh, tile], preferred_element_type=F32)
            if masked:
                st = jnp.where(causal, st, NEG)
            s_scr[par][hh, kj] = st
            slab = jnp.max(st.reshape(MAX_SLABS, t // MAX_SLABS, t), axis=0)
            m_scr[par, hh] = jnp.maximum(m_scr[par, hh], jnp.max(slab, axis=0, keepdims=True))

    def exps(par, m_fin, kj):
        for hh in heads:
            p = jnp.exp2(s_scr[par][hh, kj] - m_fin[hh])
            l_scr[par, hh] += jnp.sum(p, axis=0, keepdims=True)
            p_scr[par][hh, kj] = p.astype(BF16)

    def pv(par, kj):
        for hh in heads:
            acc_scr[hh] += jnp.dot(vt_ref[0, hh, kj], p_scr[par][hh, kj], preferred_element_type=F32)

    def start_scores(par):
        for hh in heads:
            m_scr[par, hh] = jnp.full((1, t), NEG, F32)

    def start_exps(par):
        m_fin = [m_scr[par, hh] for hh in heads]
        for hh in heads:
            l_scr[par, hh] = jnp.zeros((1, t), F32)
        return m_fin

    def start_pv():
        for hh in heads:
            acc_scr[hh] = jnp.zeros((MLA_V, t), F32)

    def finish(par, tile):
        o_t = jnp.concatenate([acc_scr[hh] / l_scr[par, hh] for hh in heads], axis=0)
        r0 = pl.multiple_of(tile * t, t)
        gate = g_ref[0, pl.ds(r0, t), :].astype(F32)
        o_ref[0, pl.ds(r0, t), :] = (o_t.T * gate).astype(BF16)

    def run_blocks(n, fns, tail):
        def span(base, width):
            for fn in fns:
                for u in range(width):
                    fn(base + u)

        def body(i, c):
            span(i * unroll, unroll)
            return c

        lax.fori_loop(0, n // unroll, body, 0)
        base = (n // unroll) * unroll
        width = unroll // 2
        while width >= TAIL_BLOCKS:
            take = (n - base) & width
            if isinstance(take, int):
                if take:
                    span(base, width)
            else:
                pl.when(take != 0)(functools.partial(span, base, width))
            base = base + take
            width //= 2
        rem = n - base

        def rest(r):
            span(base, r)
            tail()

        if isinstance(rem, int):
            rest(rem)
        else:
            for r in range(TAIL_BLOCKS):
                pl.when(rem == r)(functools.partial(rest, r))

    def step(par, qi):
        m_fin = start_exps(1 - par)
        start_scores(par)
        start_pv()

        def tail():
            exps(1 - par, m_fin, qi - 1)
            scores(par, qi, qi - 1, False)
            scores(par, qi, qi, True)
            finish(par, qi - 2)

        run_blocks(qi - 1, [functools.partial(pv, par), functools.partial(exps, 1 - par, m_fin),
                            lambda kj: scores(par, qi, kj, False)], tail)

    def step_pair(i, carry):
        step(0, 2 * i)
        step(1, 2 * i + 1)
        return carry

    start_scores(0)
    scores(0, 0, 0, True)
    m_fin = start_exps(0)
    start_scores(1)
    exps(0, m_fin, 0)
    scores(1, 1, 0, False)
    scores(1, 1, 1, True)
    lax.fori_loop(1, ns // 2, step_pair, 0)
    m_fin = start_exps(1)
    start_pv()
    run_blocks(ns - 1, [functools.partial(pv, 0), functools.partial(exps, 1, m_fin)],
               lambda: (exps(1, m_fin, ns - 1), finish(0, ns - 2)))
    start_pv()
    run_blocks(ns, [functools.partial(pv, 1)], lambda: finish(1, ns - 1))


def _mla_call(qt, k, vt, g):
    b, _, ns, _, t = qt.shape
    s = ns * t
    hp = HEADS_PER_STEP
    return pl.pallas_call(
        functools.partial(_mla_kernel, ns=ns), grid=(b, MLA_HEADS // hp),
        in_specs=[
            pl.BlockSpec((1, hp, ns, MLA_HEAD_PAD, t), lambda bi, h: (bi, h, 0, 0, 0)),
            pl.BlockSpec((1, hp, s, MLA_HEAD_PAD), lambda bi, h: (bi, h, 0, 0)),
            pl.BlockSpec((1, hp, ns, MLA_V, t), lambda bi, h: (bi, h, 0, 0, 0)),
            pl.BlockSpec((1, s, hp * MLA_V), lambda bi, h: (bi, 0, h)),
        ],
        out_specs=pl.BlockSpec((1, s, hp * MLA_V), lambda bi, h: (bi, 0, h)),
        out_shape=jax.ShapeDtypeStruct((b, s, MLA_WIDTH), BF16),
        scratch_shapes=[
            pltpu.VMEM((hp, ns, t, t), F32),
            pltpu.VMEM((hp, ns, t, t), F32),
            pltpu.VMEM((hp, ns, t, t), BF16),
            pltpu.VMEM((hp, ns, t, t), BF16),
            pltpu.VMEM((2, hp, 1, t), F32),
            pltpu.VMEM((2, hp, 1, t), F32),
            pltpu.VMEM((hp, MLA_V, t), F32),
        ],
        compiler_params=pltpu.CompilerParams(dimension_semantics=("arbitrary", "arbitrary"),
                                             vmem_limit_bytes=VMEM_LIMIT),
        name="mla_attn",
    )(qt, k, vt, g)


def _side_kernel(sinks_ref, side_ref, prev_ref, kvm_ref, x_ref, ya_ref, w_ref, fn_ref, o_ref, bias_scr, y_scr, *,
                 final):
    w = WINDOW
    tq = side_ref.shape[1]
    kj = lax.broadcasted_iota(jnp.int32, (w, 2 * w), 1)

    @pl.when((pl.program_id(0) == 0) & (pl.program_id(1) == 0))
    def _():
        qi = lax.broadcasted_iota(jnp.int32, (w, 2 * w), 0)
        dist = w + qi - kj
        in_window = (dist >= 0) & (dist < w)
        distf = dist.astype(F32)
        for head in range(SWA_HEADS):
            slope = 2.0 ** (-8.0 * (head + 1) / SWA_HEADS) * LOG2_E
            bias_scr[head] = jnp.where(in_window, -slope * distf, NEG)

    prev_ok = (kj >= w) | (pl.program_id(1) > 0)
    lane = lax.broadcasted_iota(jnp.int32, (w, LANES), 1)
    lo = lane < SWA_HEAD_DIM
    lane_kv = lax.broadcasted_iota(jnp.int32, (2 * w, LANES), 1)
    lo_kv = lane_kv < SWA_HEAD_DIM

    for n in range(tq // w):
        rows = slice(n * w, (n + 1) * w)
        kcol = slice(S_KSWA, S_KSWA + LANES)
        vcol = slice(S_VSWA, S_VSWA + LANES)
        if n == 0:
            k_prev = prev_ref[0, :, 0:LANES]
            v_prev = prev_ref[0, :, LANES:2 * LANES]
        else:
            k_prev = side_ref[0, (n - 1) * w:n * w, kcol]
            v_prev = side_ref[0, (n - 1) * w:n * w, vcol]
        k_both = jnp.concatenate([k_prev, side_ref[0, rows, kcol]], axis=0)
        v_both = jnp.concatenate([v_prev, side_ref[0, rows, vcol]], axis=0)
        k_swap = pltpu.roll(k_both, SWA_HEAD_DIM, 1)
        v_swap = pltpu.roll(v_both, SWA_HEAD_DIM, 1)
        zero_kv = jnp.zeros_like(v_both)
        for kv in range(SWA_KV_HEADS):
            own_lo, own_hi = (k_both, k_swap) if kv == 0 else (k_swap, k_both)
            kcat = jnp.where(lo_kv, own_lo, own_hi)
            v_lo = jnp.where(lo_kv, v_both if kv == 0 else v_swap, zero_kv)
            v_hi = jnp.where(lo_kv, zero_kv, v_swap if kv == 0 else v_both)
            q_parts = []
            for j in range(SWA_GROUP // 2):
                qcol = slice(S_QSWA + (kv * SWA_GROUP + 2 * j) * SWA_HEAD_DIM,
                             S_QSWA + (kv * SWA_GROUP + 2 * j + 2) * SWA_HEAD_DIM)
                qp = side_ref[0, rows, qcol]
                q_parts.append(jnp.where(lo, qp, jnp.zeros_like(qp)))
                q_parts.append(jnp.where(lo, jnp.zeros_like(qp), qp))
            qs = jnp.concatenate(q_parts, axis=0)
            sc = lax.dot_general(qs, kcat, NT_DIMS, preferred_element_type=F32)
            probs = []
            for g in range(SWA_GROUP):
                head = kv * SWA_GROUP + g
                sink = sinks_ref[head] * LOG2_E
                sg = sc[g * w:(g + 1) * w] + bias_scr[head]
                if n == 0:
                    sg = jnp.where(prev_ok, sg, NEG)
                m = jnp.maximum(jnp.max(sg, axis=-1, keepdims=True), sink)
                e = jnp.exp2(sg - m)
                den = jnp.sum(e, axis=-1, keepdims=True) + jnp.exp2(sink - m)
                probs.append((e * (1.0 / den)).astype(BF16))
            for j in range(SWA_GROUP // 2):
                out = (jnp.dot(probs[2 * j], v_lo, preferred_element_type=F32)
                       + jnp.dot(probs[2 * j + 1], v_hi, preferred_element_type=F32))
                c0 = (kv * SWA_GROUP + 2 * j) * SWA_HEAD_DIM
                gate = side_ref[0, rows, S_GSWA + c0:S_GSWA + c0 + LANES].astype(F32)
                y_scr[rows, c0:c0 + LANES] = (out * gate).astype(BF16)

    for h in range(MEM_HEADS):
        hc = slice(h * MEM_HEAD_DIM, (h + 1) * MEM_HEAD_DIM)
        q = side_ref[0, :, S_QMEM + h * MEM_HEAD_DIM:S_QMEM + (h + 1) * MEM_HEAD_DIM]
        sc = lax.dot_general(q, kvm_ref[0, :, hc], NT_DIMS, preferred_element_type=F32)
        m = jnp.max(sc, axis=-1, keepdims=True)
        e = jnp.exp2(sc - m)
        p = (e * (1.0 / jnp.sum(e, axis=-1, keepdims=True))).astype(BF16)
        out = jnp.dot(p, kvm_ref[0, :, MEM_WIDTH + h * MEM_HEAD_DIM:MEM_WIDTH + (h + 1) * MEM_HEAD_DIM],
                      preferred_element_type=F32)
        gate = side_ref[0, :, S_GMEM + h * MEM_HEAD_DIM:S_GMEM + (h + 1) * MEM_HEAD_DIM].astype(F32)
        y_scr[:, SWA_WIDTH + h * MEM_HEAD_DIM:SWA_WIDTH + (h + 1) * MEM_HEAD_DIM] = (out * gate).astype(BF16)

    y = (x_ref[0]
         + jnp.dot(ya_ref[0], w_ref[0:MLA_WIDTH, :], preferred_element_type=F32)
         + jnp.dot(y_scr[...], w_ref[MLA_WIDTH:, :], preferred_element_type=F32))
    if final:
        y = _rms(y, fn_ref[...])
    o_ref[0] = y


def _side_out_call(sinks, side, kvm, x, y_mla, w_out, fn, final):
    b, s, _ = side.shape
    tq = SIDE_TILE
    per = tq // WINDOW
    m = kvm.shape[1]
    kv_cols = 2 * SWA_KV_WIDTH
    return pl.pallas_call(
        functools.partial(_side_kernel, final=final), grid=(b, s // tq),
        in_specs=[
            pl.BlockSpec(memory_space=pltpu.SMEM),
            pl.BlockSpec((1, tq, SIDE_COLS), lambda bi, i: (bi, i, 0)),
            pl.BlockSpec((1, WINDOW, kv_cols), lambda bi, i: (bi, jnp.maximum(i * per - 1, 0), S_KSWA // kv_cols)),
            pl.BlockSpec((1, m, 2 * MEM_WIDTH), lambda bi, i: (bi, 0, 0)),
            pl.BlockSpec((1, tq, D_MODEL), lambda bi, i: (bi, i, 0)),
            pl.BlockSpec((1, tq, MLA_WIDTH), lambda bi, i: (bi, i, 0)),
            _const_spec(w_out.shape), _const_spec(fn.shape),
        ],
        out_specs=pl.BlockSpec((1, tq, D_MODEL), lambda bi, i: (bi, i, 0)),
        out_shape=jax.ShapeDtypeStruct((b, s, D_MODEL), F32),
        scratch_shapes=[
            pltpu.VMEM((SWA_HEADS, WINDOW, 2 * WINDOW), F32),
            pltpu.VMEM((tq, SWA_WIDTH + MEM_WIDTH), BF16),
        ],
        compiler_params=pltpu.CompilerParams(dimension_semantics=("arbitrary", "arbitrary"),
                                             vmem_limit_bytes=VMEM_LIMIT),
        name="side_out",
    )(sinks, side, side, kvm, x, y_mla, w_out, fn)


def _pack_w_in(w):
    c_q, c_kv, k_rope, z_mla, q_swa, k_swa, v_swa, z_swa, q_mem, z_mem = jnp.split(w, SPLIT_IDX, axis=1)
    kr = jnp.pad(k_rope, ((0, 0), (MLA_NOPE, LANES - MLA_NOPE - MLA_ROPE)))
    return jnp.concatenate([c_q, c_kv, kr, z_mla, q_swa, k_swa, v_swa, z_swa, q_mem, z_mem],
                           axis=1).astype(BF16)


def _pack_w_uq_t(w):
    return w.T.astype(BF16)


def _pack_w_ukv(w):
    w3 = w.reshape(MLA_KV_RANK, MLA_HEADS, MLA_NOPE + MLA_V)
    wk = w3[:, :, :MLA_NOPE].reshape(MLA_KV_RANK, MLA_HEADS * MLA_NOPE).astype(BF16)
    wvt = w3[:, :, MLA_NOPE:].reshape(MLA_KV_RANK, MLA_WIDTH).T.astype(BF16)
    return wk, wvt


def _rope_tables(s):
    inv = ROPE_BASE ** (-jnp.arange(0, MLA_ROPE, 2, dtype=F32) / MLA_ROPE)
    ang = jnp.arange(s, dtype=F32)[:, None] * inv[None, :]
    cos, sin = jnp.cos(ang), jnp.sin(ang)
    lead = jnp.zeros((s, MLA_NOPE), F32)
    tail = jnp.zeros((s, LANES - MLA_NOPE - MLA_ROPE), F32)
    half = jnp.zeros((s, HALF_ROPE), F32)
    cc = jnp.concatenate([lead, cos, cos, tail], axis=1)
    sa = jnp.concatenate([lead, -sin, half, tail], axis=1)
    sb = jnp.concatenate([lead, half, sin, tail], axis=1)
    return cos.T, sin.T, cc, sa, sb


def kernel(x, mem, attn_norm, w_in, mla_q_norm, w_uq, mla_kv_norm, w_ukv, swa_sinks, mem_norm, w_mem_kv, w_out,
           final_norm):
    depth = w_in.shape[0]
    s = x.shape[1]
    assert s % PROJ_TILE == 0 and PROJ_TILE % ATT_TILE == 0 and s % SIDE_TILE == 0 and s % (2 * ATT_TILE) == 0
    cost, sint, cc, sa, sb = _rope_tables(s)
    fn = final_norm.reshape(1, D_MODEL)
    for l in range(depth):
        wk, wvt = _pack_w_ukv(w_ukv[l])
        qt, k, vt, gmla, side = _proj_call(
            x, attn_norm[l].reshape(1, D_MODEL), _pack_w_in(w_in[l]), mla_q_norm[l].reshape(1, MLA_Q_RANK),
            _pack_w_uq_t(w_uq[l]), mla_kv_norm[l].reshape(1, MLA_KV_RANK), wk, wvt, cost, sint, cc, sa, sb)
        kvm = _memkv_call(mem, mem_norm[l].reshape(1, D_MODEL), w_mem_kv[l].astype(BF16))
        y_mla = _mla_call(qt, k, vt, gmla)
        x = _side_out_call(swa_sinks[l], side, kvm, x, y_mla, w_out[l].astype(BF16), fn, final=(l == depth - 1))
    return x
```

```python
import functools

import jax
import jax.numpy as jnp
import numpy as np
from jax import lax
from jax.experimental import pallas as pl
from jax.experimental.pallas import tpu as pltpu

F32 = jnp.float32
BF16 = jnp.bfloat16

D_MODEL = 1024
MLA_HEADS = 16
MLA_NOPE = 64
MLA_ROPE = 32
MLA_V = 64
MLA_Q_RANK = 384
MLA_KV_RANK = 256
ROPE_BASE = 10000.0
SWA_HEADS = 8
SWA_KV_HEADS = 2
SWA_HEAD_DIM = 64
SWA_GROUP = SWA_HEADS // SWA_KV_HEADS
WINDOW = 128
MEM_HEADS = 4
MEM_HEAD_DIM = 128
EPS = 1e-6
NEG = -1e30
LOG2_E = 1.4426950408889634

MLA_WIDTH = MLA_HEADS * MLA_V
SWA_WIDTH = SWA_HEADS * SWA_HEAD_DIM
SWA_KV_WIDTH = SWA_KV_HEADS * SWA_HEAD_DIM
MEM_WIDTH = MEM_HEADS * MEM_HEAD_DIM
SPLITS = (MLA_Q_RANK, MLA_KV_RANK, MLA_ROPE, MLA_WIDTH, SWA_WIDTH, SWA_KV_WIDTH, SWA_KV_WIDTH,
          SWA_WIDTH, MEM_WIDTH, MEM_WIDTH)
SPLIT_IDX = tuple(int(i) for i in np.cumsum(SPLITS)[:-1])

LANES = 128
MLA_HEAD_PAD = 128
HALF_ROPE = MLA_ROPE // 2
ATT_TILE = 256
PROJ_TILE = 512
SIDE_TILE = 512
VMEM_LIMIT = 56 * 1024 * 1024

C_Q = 0
C_KV = C_Q + MLA_Q_RANK
C_KR = C_KV + MLA_KV_RANK
C_ZMLA = C_KR + LANES
C_QSWA = C_ZMLA + MLA_WIDTH
C_KSWA = C_QSWA + SWA_WIDTH
C_VSWA = C_KSWA + SWA_KV_WIDTH
C_ZSWA = C_VSWA + SWA_KV_WIDTH
C_QMEM = C_ZSWA + SWA_WIDTH
C_ZMEM = C_QMEM + MEM_WIDTH
W_IN_COLS = C_ZMEM + MEM_WIDTH

S_QSWA = 0
S_KSWA = S_QSWA + SWA_WIDTH
S_VSWA = S_KSWA + SWA_KV_WIDTH
S_GSWA = S_VSWA + SWA_KV_WIDTH
S_QMEM = S_GSWA + SWA_WIDTH
S_GMEM = S_QMEM + MEM_WIDTH
SIDE_COLS = S_GMEM + MEM_WIDTH

NT_DIMS = (((1,), (1,)), ((), ()))


def _rms(x, g):
    return x * lax.rsqrt(jnp.mean(x * x, axis=-1, keepdims=True) + EPS) * g


def _silu(z):
    return z * jax.nn.sigmoid(z)


def _const_spec(shape):
    nd = len(shape)
    return pl.BlockSpec(shape, lambda *_: (0,) * nd, pipeline_mode=pl.Buffered(1))


def _proj_kernel(x_ref, an_ref, win_ref, qn_ref, wuqt_ref, kvn_ref, wuk_ref, wuvt_ref,
                 cost_ref, sint_ref, cc_ref, sa_ref, sb_ref,
                 qt_ref, k_ref, vt_ref, gmla_ref, side_ref):
    hn = _rms(x_ref[0], an_ref[...]).astype(BF16)

    def proj(a, b):
        return jnp.dot(hn, win_ref[:, a:b], preferred_element_type=F32)

    cqn = _rms(proj(C_Q, C_KV), qn_ref[...]).astype(BF16)
    scale = (MLA_NOPE + MLA_ROPE) ** -0.5 * LOG2_E
    qt = lax.dot_general(wuqt_ref[...], cqn, NT_DIMS, preferred_element_type=F32) * scale
    c = cost_ref[...]
    s = sint_ref[...]
    zero_rows = jnp.zeros((MLA_HEAD_PAD - MLA_NOPE - MLA_ROPE, ATT_TILE), BF16)
    att_tiles = [slice(a * ATT_TILE, (a + 1) * ATT_TILE) for a in range(qt.shape[1] // ATT_TILE)]
    for h in range(MLA_HEADS):
        r = h * (MLA_NOPE + MLA_ROPE)
        x1 = qt[r + MLA_NOPE:r + MLA_NOPE + HALF_ROPE]
        x2 = qt[r + MLA_NOPE + HALF_ROPE:r + MLA_NOPE + MLA_ROPE]
        nope = qt[r:r + MLA_NOPE].astype(BF16)
        r1 = (x1 * c - x2 * s).astype(BF16)
        r2 = (x1 * s + x2 * c).astype(BF16)
        for a, cols in enumerate(att_tiles):
            qt_ref[0, h, a, 0:MLA_NOPE, :] = nope[:, cols]
            qt_ref[0, h, a, MLA_NOPE:MLA_NOPE + HALF_ROPE, :] = r1[:, cols]
            qt_ref[0, h, a, MLA_NOPE + HALF_ROPE:MLA_NOPE + MLA_ROPE, :] = r2[:, cols]
            qt_ref[0, h, a, MLA_NOPE + MLA_ROPE:MLA_HEAD_PAD, :] = zero_rows

    ckvn = _rms(proj(C_KV, C_KR), kvn_ref[...]).astype(BF16)
    kr = proj(C_KR, C_ZMLA)
    kpe = (kr * cc_ref[...] + pltpu.roll(kr, LANES - HALF_ROPE, 1) * sa_ref[...]
           + pltpu.roll(kr, HALF_ROPE, 1) * sb_ref[...])
    kk = jnp.dot(ckvn, wuk_ref[...], preferred_element_type=F32)
    lo = lax.broadcasted_iota(jnp.int32, kpe.shape, 1) < MLA_NOPE
    for h2 in range(MLA_HEADS // 2):
        pair = kk[:, h2 * LANES:(h2 + 1) * LANES]
        k_ref[0, 2 * h2, :, :] = jnp.where(lo, pair, kpe).astype(BF16)
        k_ref[0, 2 * h2 + 1, :, :] = jnp.where(lo, pltpu.roll(pair, MLA_NOPE, 1), kpe).astype(BF16)

    vt = lax.dot_general(wuvt_ref[...], ckvn, NT_DIMS, preferred_element_type=F32)
    for h in range(MLA_HEADS):
        for a, cols in enumerate(att_tiles):
            vt_ref[0, h, a, :, :] = vt[h * MLA_V:(h + 1) * MLA_V, cols].astype(BF16)

    gmla_ref[0] = _silu(proj(C_ZMLA, C_QSWA)).astype(BF16)
    side_ref[0, :, S_QSWA:S_KSWA] = (proj(C_QSWA, C_KSWA) * (SWA_HEAD_DIM ** -0.5 * LOG2_E)).astype(BF16)
    side_ref[0, :, S_KSWA:S_GSWA] = proj(C_KSWA, C_ZSWA).astype(BF16)
    side_ref[0, :, S_GSWA:S_QMEM] = _silu(proj(C_ZSWA, C_QMEM)).astype(BF16)
    side_ref[0, :, S_QMEM:S_GMEM] = (proj(C_QMEM, C_ZMEM) * (MEM_HEAD_DIM ** -0.5 * LOG2_E)).astype(BF16)
    side_ref[0, :, S_GMEM:SIDE_COLS] = _silu(proj(C_ZMEM, W_IN_COLS)).astype(BF16)


def _proj_call(x, an, win, qn, wuqt, kvn, wuk, wuvt, cost, sint, cc, sa, sb):
    b, s, _ = x.shape
    tm = PROJ_TILE
    per = tm // ATT_TILE
    ns = s // ATT_TILE
    in_specs = [
        pl.BlockSpec((1, tm, D_MODEL), lambda bi, i: (bi, i, 0)),
        _const_spec(an.shape), _const_spec(win.shape), _const_spec(qn.shape), _const_spec(wuqt.shape),
        _const_spec(kvn.shape), _const_spec(wuk.shape), _const_spec(wuvt.shape),
        pl.BlockSpec((HALF_ROPE, tm), lambda bi, i: (0, i)),
        pl.BlockSpec((HALF_ROPE, tm), lambda bi, i: (0, i)),
        pl.BlockSpec((tm, LANES), lambda bi, i: (i, 0)),
        pl.BlockSpec((tm, LANES), lambda bi, i: (i, 0)),
        pl.BlockSpec((tm, LANES), lambda bi, i: (i, 0)),
    ]
    out_shape = [
        jax.ShapeDtypeStruct((b, MLA_HEADS, ns, MLA_HEAD_PAD, ATT_TILE), BF16),
        jax.ShapeDtypeStruct((b, MLA_HEADS, s, MLA_HEAD_PAD), BF16),
        jax.ShapeDtypeStruct((b, MLA_HEADS, ns, MLA_V, ATT_TILE), BF16),
        jax.ShapeDtypeStruct((b, s, MLA_WIDTH), BF16),
        jax.ShapeDtypeStruct((b, s, SIDE_COLS), BF16),
    ]
    out_specs = [
        pl.BlockSpec((1, MLA_HEADS, per, MLA_HEAD_PAD, ATT_TILE), lambda bi, i: (bi, 0, i, 0, 0)),
        pl.BlockSpec((1, MLA_HEADS, tm, MLA_HEAD_PAD), lambda bi, i: (bi, 0, i, 0)),
        pl.BlockSpec((1, MLA_HEADS, per, MLA_V, ATT_TILE), lambda bi, i: (bi, 0, i, 0, 0)),
        pl.BlockSpec((1, tm, MLA_WIDTH), lambda bi, i: (bi, i, 0)),
        pl.BlockSpec((1, tm, SIDE_COLS), lambda bi, i: (bi, i, 0)),
    ]
    return pl.pallas_call(
        _proj_kernel, grid=(b, s // tm), in_specs=in_specs, out_specs=out_specs, out_shape=out_shape,
        compiler_params=pltpu.CompilerParams(dimension_semantics=("arbitrary", "arbitrary"),
                                             vmem_limit_bytes=VMEM_LIMIT),
        name="proj",
    )(x, an, win, qn, wuqt, kvn, wuk, wuvt, cost, sint, cc, sa, sb)


def _memkv_kernel(mem_ref, mn_ref, w_ref, o_ref):
    mn = _rms(mem_ref[0], mn_ref[...]).astype(BF16)
    o_ref[0] = jnp.dot(mn, w_ref[...], preferred_element_type=F32).astype(BF16)


def _memkv_call(mem, mn, w):
    b, m, _ = mem.shape
    return pl.pallas_call(
        _memkv_kernel, grid=(b,),
        in_specs=[pl.BlockSpec((1, m, D_MODEL), lambda bi: (bi, 0, 0)), _const_spec(mn.shape), _const_spec(w.shape)],
        out_specs=pl.BlockSpec((1, m, 2 * MEM_WIDTH), lambda bi: (bi, 0, 0)),
        out_shape=jax.ShapeDtypeStruct((b, m, 2 * MEM_WIDTH), BF16),
        compiler_params=pltpu.CompilerParams(dimension_semantics=("arbitrary",), vmem_limit_bytes=VMEM_LIMIT),
        name="memkv",
    )(mem, mn, w)


HEADS_PER_STEP = 2
MAX_SLABS = 4
TAIL_BLOCKS = 4
MLA_UNROLL = 8


def _head_rows(dtype):
    sublane_tile = 8 * 4 // jnp.dtype(dtype).itemsize
    return ATT_TILE + sublane_tile


def _mla_kernel(qt_ref, k_ref, vt_ref, g_ref, o_ref, s0, s1, p0, p1, m_scr, l_scr, acc_scr, *, ns):
    t = ATT_TILE
    heads = range(HEADS_PER_STEP)
    unroll = MLA_UNROLL
    s_scr = (s0, s1)
    p_scr = (p0, p1)
    s_rows = [pl.ds(hh * _head_rows(F32), t) for hh in heads]
    p_rows = [pl.ds(hh * _head_rows(BF16), t) for hh in heads]
    key_idx = lax.broadcasted_iota(jnp.int32, (t, t), 0)
    qry_idx = lax.broadcasted_iota(jnp.int32, (t, t), 1)
    causal = key_idx <= qry_idx

    def scores(par, tile, kj, masked):
        for hh in heads:
            kb = k_ref[0, hh, pl.ds(pl.multiple_of(kj * t, t), t), :]
            st = jnp.dot(kb, qt_ref[0, hh, tile], preferred_element_type=F32)
            if masked:
                st = jnp.where(causal, st, NEG)
            s_scr[par][kj, s_rows[hh], :] = st
            slab = jnp.max(st.reshape(MAX_SLABS, t // MAX_SLABS, t), axis=0)
            m_scr[par, hh] = jnp.maximum(m_scr[par, hh], jnp.max(slab, axis=0, keepdims=True))

    def exps(par, m_fin, kj):
        for hh in heads:
            p = jnp.exp2(s_scr[par][kj, s_rows[hh], :] - m_fin[hh])
            l_scr[par, hh] += jnp.sum(p, axis=0, keepdims=True)
            p_scr[par][kj, p_rows[hh], :] = p.astype(BF16)

    def pv(par, kj):
        for hh in heads:
            acc_scr[hh] += jnp.dot(vt_ref[0, hh, kj], p_scr[par][kj, p_rows[hh], :], preferred_element_type=F32)

    def start_scores(par):
        for hh in heads:
            m_scr[par, hh] = jnp.full((1, t), NEG, F32)

    def start_exps(par):
        m_fin = [m_scr[par, hh] for hh in heads]
        for hh in heads:
            l_scr[par, hh] = jnp.zeros((1, t), F32)
        return m_fin

    def start_pv():
        for hh in heads:
            acc_scr[hh] = jnp.zeros((MLA_V, t), F32)

    def finish(par, tile):
        o_t = jnp.concatenate([acc_scr[hh] / l_scr[par, hh] for hh in heads], axis=0)
        r0 = pl.multiple_of(tile * t, t)
        gate = g_ref[0, pl.ds(r0, t), :].astype(F32)
        o_ref[0, pl.ds(r0, t), :] = (o_t.T * gate).astype(BF16)

    def run_blocks(n, fns, tail):
        def span(base, width):
            for fn in fns:
                for u in range(width):
                    fn(base + u)

        def body(i, c):
            span(i * unroll, unroll)
            return c

        lax.fori_loop(0, n // unroll, body, 0)
        base = (n // unroll) * unroll
        width = unroll // 2
        while width >= TAIL_BLOCKS:
            take = (n - base) & width
            if isinstance(take, int):
                if take:
                    span(base, width)
            else:
                pl.when(take != 0)(functools.partial(span, base, width))
            base = base + take
            width //= 2
        rem = n - base

        def rest(r):
            span(base, r)
            tail()

        if isinstance(rem, int):
            rest(rem)
        else:
            for r in range(TAIL_BLOCKS):
                pl.when(rem == r)(functools.partial(rest, r))

    def step(par, qi):
        m_fin = start_exps(1 - par)
        start_scores(par)
        start_pv()

        def tail():
            exps(1 - par, m_fin, qi - 1)
            scores(par, qi, qi - 1, False)
            scores(par, qi, qi, True)
            finish(par, qi - 2)

        run_blocks(qi - 1, [functools.partial(pv, par), functools.partial(exps, 1 - par, m_fin),
                            lambda kj: scores(par, qi, kj, False)], tail)

    def step_pair(i, carry):
        step(0, 2 * i)
        step(1, 2 * i + 1)
        return carry

    start_scores(0)
    scores(0, 0, 0, True)
    m_fin = start_exps(0)
    start_scores(1)
    exps(0, m_fin, 0)
    scores(1, 1, 0, False)
    scores(1, 1, 1, True)
    lax.fori_loop(1, ns // 2, step_pair, 0)
    m_fin = start_exps(1)
    start_pv()
    run_blocks(ns - 1, [functools.partial(pv, 0), functools.partial(exps, 1, m_fin)],
               lambda: (exps(1, m_fin, ns - 1), finish(0, ns - 2)))
    start_pv()
    run_blocks(ns, [functools.partial(pv, 1)], lambda: finish(1, ns - 1))


def _mla_call(qt, k, vt, g):
    b, _, ns, _, t = qt.shape
    s = ns * t
    hp = HEADS_PER_STEP
    return pl.pallas_call(
        functools.partial(_mla_kernel, ns=ns), grid=(b, MLA_HEADS // hp),
        in_specs=[
            pl.BlockSpec((1, hp, ns, MLA_HEAD_PAD, t), lambda bi, h: (bi, h, 0, 0, 0)),
            pl.BlockSpec((1, hp, s, MLA_HEAD_PAD), lambda bi, h: (bi, h, 0, 0)),
            pl.BlockSpec((1, hp, ns, MLA_V, t), lambda bi, h: (bi, h, 0, 0, 0)),
            pl.BlockSpec((1, s, hp * MLA_V), lambda bi, h: (bi, 0, h)),
        ],
        out_specs=pl.BlockSpec((1, s, hp * MLA_V), lambda bi, h: (bi, 0, h)),
        out_shape=jax.ShapeDtypeStruct((b, s, MLA_WIDTH), BF16),
        scratch_shapes=[
            pltpu.VMEM((ns, hp * _head_rows(F32), t), F32),
            pltpu.VMEM((ns, hp * _head_rows(F32), t), F32),
            pltpu.VMEM((ns, hp * _head_rows(BF16), t), BF16),
            pltpu.VMEM((ns, hp * _head_rows(BF16), t), BF16),
            pltpu.VMEM((2, hp, 1, t), F32),
            pltpu.VMEM((2, hp, 1, t), F32),
            pltpu.VMEM((hp, MLA_V, t), F32),
        ],
        compiler_params=pltpu.CompilerParams(dimension_semantics=("arbitrary", "arbitrary"),
                                             vmem_limit_bytes=VMEM_LIMIT),
        name="mla_attn",
    )(qt, k, vt, g)


def _side_kernel(sinks_ref, side_ref, prev_ref, kvm_ref, x_ref, ya_ref, w_ref, fn_ref, o_ref, bias_scr, y_scr, *,
                 final):
    w = WINDOW
    tq = side_ref.shape[1]
    kj = lax.broadcasted_iota(jnp.int32, (w, 2 * w), 1)

    @pl.when((pl.program_id(0) == 0) & (pl.program_id(1) == 0))
    def _():
        qi = lax.broadcasted_iota(jnp.int32, (w, 2 * w), 0)
        dist = w + qi - kj
        in_window = (dist >= 0) & (dist < w)
        distf = dist.astype(F32)
        for head in range(SWA_HEADS):
            slope = 2.0 ** (-8.0 * (head + 1) / SWA_HEADS) * LOG2_E
            bias_scr[head] = jnp.where(in_window, -slope * distf, NEG)

    prev_ok = (kj >= w) | (pl.program_id(1) > 0)
    lane = lax.broadcasted_iota(jnp.int32, (w, LANES), 1)
    lo = lane < SWA_HEAD_DIM
    lane_kv = lax.broadcasted_iota(jnp.int32, (2 * w, LANES), 1)
    lo_kv = lane_kv < SWA_HEAD_DIM

    for n in range(tq // w):
        rows = slice(n * w, (n + 1) * w)
        kcol = slice(S_KSWA, S_KSWA + LANES)
        vcol = slice(S_VSWA, S_VSWA + LANES)
        if n == 0:
            k_prev = prev_ref[0, :, 0:LANES]
            v_prev = prev_ref[0, :, LANES:2 * LANES]
        else:
            k_prev = side_ref[0, (n - 1) * w:n * w, kcol]
            v_prev = side_ref[0, (n - 1) * w:n * w, vcol]
        k_both = jnp.concatenate([k_prev, side_ref[0, rows, kcol]], axis=0)
        v_both = jnp.concatenate([v_prev, side_ref[0, rows, vcol]], axis=0)
        k_swap = pltpu.roll(k_both, SWA_HEAD_DIM, 1)
        v_swap = pltpu.roll(v_both, SWA_HEAD_DIM, 1)
        zero_kv = jnp.zeros_like(v_both)
        for kv in range(SWA_KV_HEADS):
            own_lo, own_hi = (k_both, k_swap) if kv == 0 else (k_swap, k_both)
            kcat = jnp.where(lo_kv, own_lo, own_hi)
            v_lo = jnp.where(lo_kv, v_both if kv == 0 else v_swap, zero_kv)
            v_hi = jnp.where(lo_kv, zero_kv, v_swap if kv == 0 else v_both)
            q_parts = []
            for j in range(SWA_GROUP // 2):
                qcol = slice(S_QSWA + (kv * SWA_GROUP + 2 * j) * SWA_HEAD_DIM,
                             S_QSWA + (kv * SWA_GROUP + 2 * j + 2) * SWA_HEAD_DIM)
                qp = side_ref[0, rows, qcol]
                q_parts.append(jnp.where(lo, qp, jnp.zeros_like(qp)))
                q_parts.append(jnp.where(lo, jnp.zeros_like(qp), qp))
            qs = jnp.concatenate(q_parts, axis=0)
            sc = lax.dot_general(qs, kcat, NT_DIMS, preferred_element_type=F32)
            probs = []
            for g in range(SWA_GROUP):
                head = kv * SWA_GROUP + g
                sink = sinks_ref[head] * LOG2_E
                sg = sc[g * w:(g + 1) * w] + bias_scr[head]
                if n == 0:
                    sg = jnp.where(prev_ok, sg, NEG)
                m = jnp.maximum(jnp.max(sg, axis=-1, keepdims=True), sink)
                e = jnp.exp2(sg - m)
                den = jnp.sum(e, axis=-1, keepdims=True) + jnp.exp2(sink - m)
                probs.append((e * (1.0 / den)).astype(BF16))
            for j in range(SWA_GROUP // 2):
                out = (jnp.dot(probs[2 * j], v_lo, preferred_element_type=F32)
                       + jnp.dot(probs[2 * j + 1], v_hi, preferred_element_type=F32))
                c0 = (kv * SWA_GROUP + 2 * j) * SWA_HEAD_DIM
                gate = side_ref[0, rows, S_GSWA + c0:S_GSWA + c0 + LANES].astype(F32)
                y_scr[rows, c0:c0 + LANES] = (out * gate).astype(BF16)

    for h in range(MEM_HEADS):
        hc = slice(h * MEM_HEAD_DIM, (h + 1) * MEM_HEAD_DIM)
        q = side_ref[0, :, S_QMEM + h * MEM_HEAD_DIM:S_QMEM + (h + 1) * MEM_HEAD_DIM]
        sc = lax.dot_general(q, kvm_ref[0, :, hc], NT_DIMS, preferred_element_type=F32)
        m = jnp.max(sc, axis=-1, keepdims=True)
        e = jnp.exp2(sc - m)
        p = (e * (1.0 / jnp.sum(e, axis=-1, keepdims=True))).astype(BF16)
        out = jnp.dot(p, kvm_ref[0, :, MEM_WIDTH + h * MEM_HEAD_DIM:MEM_WIDTH + (h + 1) * MEM_HEAD_DIM],
                      preferred_element_type=F32)
        gate = side_ref[0, :, S_GMEM + h * MEM_HEAD_DIM:S_GMEM + (h + 1) * MEM_HEAD_DIM].astype(F32)
        y_scr[:, SWA_WIDTH + h * MEM_HEAD_DIM:SWA_WIDTH + (h + 1) * MEM_HEAD_DIM] = (out * gate).astype(BF16)

    y = (x_ref[0]
         + jnp.dot(ya_ref[0], w_ref[0:MLA_WIDTH, :], preferred_element_type=F32)
         + jnp.dot(y_scr[...], w_ref[MLA_WIDTH:, :], preferred_element_type=F32))
    if final:
        y = _rms(y, fn_ref[...])
    o_ref[0] = y


def _side_out_call(sinks, side, kvm, x, y_mla, w_out, fn, final):
    b, s, _ = side.shape
    tq = SIDE_TILE
    per = tq // WINDOW
    m = kvm.shape[1]
    kv_cols = 2 * SWA_KV_WIDTH
    return pl.pallas_call(
        functools.partial(_side_kernel, final=final), grid=(b, s // tq),
        in_specs=[
            pl.BlockSpec(memory_space=pltpu.SMEM),
            pl.BlockSpec((1, tq, SIDE_COLS), lambda bi, i: (bi, i, 0)),
            pl.BlockSpec((1, WINDOW, kv_cols), lambda bi, i: (bi, jnp.maximum(i * per - 1, 0), S_KSWA // kv_cols)),
            pl.BlockSpec((1, m, 2 * MEM_WIDTH), lambda bi, i: (bi, 0, 0)),
            pl.BlockSpec((1, tq, D_MODEL), lambda bi, i: (bi, i, 0)),
            pl.BlockSpec((1, tq, MLA_WIDTH), lambda bi, i: (bi, i, 0)),
            _const_spec(w_out.shape), _const_spec(fn.shape),
        ],
        out_specs=pl.BlockSpec((1, tq, D_MODEL), lambda bi, i: (bi, i, 0)),
        out_shape=jax.ShapeDtypeStruct((b, s, D_MODEL), F32),
        scratch_shapes=[
            pltpu.VMEM((SWA_HEADS, WINDOW, 2 * WINDOW), F32),
            pltpu.VMEM((tq, SWA_WIDTH + MEM_WIDTH), BF16),
        ],
        compiler_params=pltpu.CompilerParams(dimension_semantics=("arbitrary", "arbitrary"),
                                             vmem_limit_bytes=VMEM_LIMIT),
        name="side_out",
    )(sinks, side, side, kvm, x, y_mla, w_out, fn)


def _pack_w_in(w):
    c_q, c_kv, k_rope, z_mla, q_swa, k_swa, v_swa, z_swa, q_mem, z_mem = jnp.split(w, SPLIT_IDX, axis=1)
    kr = jnp.pad(k_rope, ((0, 0), (MLA_NOPE, LANES - MLA_NOPE - MLA_ROPE)))
    return jnp.concatenate([c_q, c_kv, kr, z_mla, q_swa, k_swa, v_swa, z_swa, q_mem, z_mem],
                           axis=1).astype(BF16)


def _pack_w_uq_t(w):
    return w.T.astype(BF16)


def _pack_w_ukv(w):
    w3 = w.reshape(MLA_KV_RANK, MLA_HEADS, MLA_NOPE + MLA_V)
    wk = w3[:, :, :MLA_NOPE].reshape(MLA_KV_RANK, MLA_HEADS * MLA_NOPE).astype(BF16)
    wvt = w3[:, :, MLA_NOPE:].reshape(MLA_KV_RANK, MLA_WIDTH).T.astype(BF16)
    return wk, wvt


def _rope_tables(s):
    inv = ROPE_BASE ** (-jnp.arange(0, MLA_ROPE, 2, dtype=F32) / MLA_ROPE)
    ang = jnp.arange(s, dtype=F32)[:, None] * inv[None, :]
    cos, sin = jnp.cos(ang), jnp.sin(ang)
    lead = jnp.zeros((s, MLA_NOPE), F32)
    tail = jnp.zeros((s, LANES - MLA_NOPE - MLA_ROPE), F32)
    half = jnp.zeros((s, HALF_ROPE), F32)
    cc = jnp.concatenate([lead, cos, cos, tail], axis=1)
    sa = jnp.concatenate([lead, -sin, half, tail], axis=1)
    sb = jnp.concatenate([lead, half, sin, tail], axis=1)
    return cos.T, sin.T, cc, sa, sb


def kernel(x, mem, attn_norm, w_in, mla_q_norm, w_uq, mla_kv_norm, w_ukv, swa_sinks, mem_norm, w_mem_kv, w_out,
           final_norm):
    depth = w_in.shape[0]
    s = x.shape[1]
    assert s % PROJ_TILE == 0 and PROJ_TILE % ATT_TILE == 0 and s % SIDE_TILE == 0 and s % (2 * ATT_TILE) == 0
    cost, sint, cc, sa, sb = _rope_tables(s)
    fn = final_norm.reshape(1, D_MODEL)
    for l in range(depth):
        wk, wvt = _pack_w_ukv(w_ukv[l])
        qt, k, vt, gmla, side = _proj_call(
            x, attn_norm[l].reshape(1, D_MODEL), _pack_w_in(w_in[l]), mla_q_norm[l].reshape(1, MLA_Q_RANK),
            _pack_w_uq_t(w_uq[l]), mla_kv_norm[l].reshape(1, MLA_KV_RANK), wk, wvt, cost, sint, cc, sa, sb)
        kvm = _memkv_call(mem, mem_norm[l].reshape(1, D_MODEL), w_mem_kv[l].astype(BF16))
        y_mla = _mla_call(qt, k, vt, gmla)
        x = _side_out_call(swa_sinks[l], side, kvm, x, y_mla, w_out[l].astype(BF16), fn, final=(l == depth - 1))
    return x
```

```python
import functools

import jax
import jax.numpy as jnp
import numpy as np
from jax import lax
from jax.experimental import pallas as pl
from jax.experimental.pallas import tpu as pltpu

F32 = jnp.float32
BF16 = jnp.bfloat16

D_MODEL = 1024
MLA_HEADS = 16
MLA_NOPE = 64
MLA_ROPE = 32
MLA_V = 64
MLA_Q_RANK = 384
MLA_KV_RANK = 256
ROPE_BASE = 10000.0
SWA_HEADS = 8
SWA_KV_HEADS = 2
SWA_HEAD_DIM = 64
SWA_GROUP = SWA_HEADS // SWA_KV_HEADS
WINDOW = 128
MEM_HEADS = 4
MEM_HEAD_DIM = 128
EPS = 1e-6
NEG = -1e30
LOG2_E = 1.4426950408889634

MLA_WIDTH = MLA_HEADS * MLA_V
SWA_WIDTH = SWA_HEADS * SWA_HEAD_DIM
SWA_KV_WIDTH = SWA_KV_HEADS * SWA_HEAD_DIM
MEM_WIDTH = MEM_HEADS * MEM_HEAD_DIM
SPLITS = (MLA_Q_RANK, MLA_KV_RANK, MLA_ROPE, MLA_WIDTH, SWA_WIDTH, SWA_KV_WIDTH, SWA_KV_WIDTH,
          SWA_WIDTH, MEM_WIDTH, MEM_WIDTH)
SPLIT_IDX = tuple(int(i) for i in np.cumsum(SPLITS)[:-1])

LANES = 128
MLA_HEAD_PAD = 128
HALF_ROPE = MLA_ROPE // 2
ATT_TILE = 256
PROJ_TILE = 512
SIDE_TILE = 512
VMEM_LIMIT = 56 * 1024 * 1024

C_Q = 0
C_KV = C_Q + MLA_Q_RANK
C_KR = C_KV + MLA_KV_RANK
C_ZMLA = C_KR + LANES
C_QSWA = C_ZMLA + MLA_WIDTH
C_KSWA = C_QSWA + SWA_WIDTH
C_VSWA = C_KSWA + SWA_KV_WIDTH
C_ZSWA = C_VSWA + SWA_KV_WIDTH
C_QMEM = C_ZSWA + SWA_WIDTH
C_ZMEM = C_QMEM + MEM_WIDTH
W_IN_COLS = C_ZMEM + MEM_WIDTH

S_QSWA = 0
S_KSWA = S_QSWA + SWA_WIDTH
S_VSWA = S_KSWA + SWA_KV_WIDTH
S_GSWA = S_VSWA + SWA_KV_WIDTH
S_QMEM = S_GSWA + SWA_WIDTH
S_GMEM = S_QMEM + MEM_WIDTH
SIDE_COLS = S_GMEM + MEM_WIDTH

NT_DIMS = (((1,), (1,)), ((), ()))


def _rms(x, g):
    return x * lax.rsqrt(jnp.mean(x * x, axis=-1, keepdims=True) + EPS) * g


def _silu(z):
    return z * jax.nn.sigmoid(z)


def _const_spec(shape):
    nd = len(shape)
    return pl.BlockSpec(shape, lambda *_: (0,) * nd, pipeline_mode=pl.Buffered(1))


def _proj_kernel(x_ref, an_ref, win_ref, qn_ref, wuqt_ref, kvn_ref, wuk_ref, wuvt_ref,
                 cost_ref, sint_ref, cc_ref, sa_ref, sb_ref,
                 qt_ref, k_ref, vt_ref, gmla_ref, side_ref):
    hn = _rms(x_ref[0], an_ref[...]).astype(BF16)

    def proj(a, b):
        return jnp.dot(hn, win_ref[:, a:b], preferred_element_type=F32)

    cqn = _rms(proj(C_Q, C_KV), qn_ref[...]).astype(BF16)
    scale = (MLA_NOPE + MLA_ROPE) ** -0.5 * LOG2_E
    qt = lax.dot_general(wuqt_ref[...], cqn, NT_DIMS, preferred_element_type=F32) * scale
    c = cost_ref[...]
    s = sint_ref[...]
    zero_rows = jnp.zeros((MLA_HEAD_PAD - MLA_NOPE - MLA_ROPE, ATT_TILE), BF16)
    att_tiles = [slice(a * ATT_TILE, (a + 1) * ATT_TILE) for a in range(qt.shape[1] // ATT_TILE)]
    for h in range(MLA_HEADS):
        r = h * (MLA_NOPE + MLA_ROPE)
        x1 = qt[r + MLA_NOPE:r + MLA_NOPE + HALF_ROPE]
        x2 = qt[r + MLA_NOPE + HALF_ROPE:r + MLA_NOPE + MLA_ROPE]
        nope = qt[r:r + MLA_NOPE].astype(BF16)
        r1 = (x1 * c - x2 * s).astype(BF16)
        r2 = (x1 * s + x2 * c).astype(BF16)
        for a, cols in enumerate(att_tiles):
            qt_ref[0, h, a, 0:MLA_NOPE, :] = nope[:, cols]
            qt_ref[0, h, a, MLA_NOPE:MLA_NOPE + HALF_ROPE, :] = r1[:, cols]
            qt_ref[0, h, a, MLA_NOPE + HALF_ROPE:MLA_NOPE + MLA_ROPE, :] = r2[:, cols]
            qt_ref[0, h, a, MLA_NOPE + MLA_ROPE:MLA_HEAD_PAD, :] = zero_rows

    ckvn = _rms(proj(C_KV, C_KR), kvn_ref[...]).astype(BF16)
    kr = proj(C_KR, C_ZMLA)
    kpe = (kr * cc_ref[...] + pltpu.roll(kr, LANES - HALF_ROPE, 1) * sa_ref[...]
           + pltpu.roll(kr, HALF_ROPE, 1) * sb_ref[...])
    kk = jnp.dot(ckvn, wuk_ref[...], preferred_element_type=F32)
    lo = lax.broadcasted_iota(jnp.int32, kpe.shape, 1) < MLA_NOPE
    for h2 in range(MLA_HEADS // 2):
        pair = kk[:, h2 * LANES:(h2 + 1) * LANES]
        k_ref[0, 2 * h2, :, :] = jnp.where(lo, pair, kpe).astype(BF16)
        k_ref[0, 2 * h2 + 1, :, :] = jnp.where(lo, pltpu.roll(pair, MLA_NOPE, 1), kpe).astype(BF16)

    vt = lax.dot_general(wuvt_ref[...], ckvn, NT_DIMS, preferred_element_type=F32)
    for h in range(MLA_HEADS):
        for a, cols in enumerate(att_tiles):
            vt_ref[0, h, a, :, :] = vt[h * MLA_V:(h + 1) * MLA_V, cols].astype(BF16)

    gmla_ref[0] = _silu(proj(C_ZMLA, C_QSWA)).astype(BF16)
    side_ref[0, :, S_QSWA:S_KSWA] = (proj(C_QSWA, C_KSWA) * (SWA_HEAD_DIM ** -0.5 * LOG2_E)).astype(BF16)
    side_ref[0, :, S_KSWA:S_GSWA] = proj(C_KSWA, C_ZSWA).astype(BF16)
    side_ref[0, :, S_GSWA:S_QMEM] = _silu(proj(C_ZSWA, C_QMEM)).astype(BF16)
    side_ref[0, :, S_QMEM:S_GMEM] = (proj(C_QMEM, C_ZMEM) * (MEM_HEAD_DIM ** -0.5 * LOG2_E)).astype(BF16)
    side_ref[0, :, S_GMEM:SIDE_COLS] = _silu(proj(C_ZMEM, W_IN_COLS)).astype(BF16)


def _proj_call(x, an, win, qn, wuqt, kvn, wuk, wuvt, cost, sint, cc, sa, sb):
    b, s, _ = x.shape
    tm = PROJ_TILE
    per = tm // ATT_TILE
    ns = s // ATT_TILE
    in_specs = [
        pl.BlockSpec((1, tm, D_MODEL), lambda bi, i: (bi, i, 0)),
        _const_spec(an.shape), _const_spec(win.shape), _const_spec(qn.shape), _const_spec(wuqt.shape),
        _const_spec(kvn.shape), _const_spec(wuk.shape), _const_spec(wuvt.shape),
        pl.BlockSpec((HALF_ROPE, tm), lambda bi, i: (0, i)),
        pl.BlockSpec((HALF_ROPE, tm), lambda bi, i: (0, i)),
        pl.BlockSpec((tm, LANES), lambda bi, i: (i, 0)),
        pl.BlockSpec((tm, LANES), lambda bi, i: (i, 0)),
        pl.BlockSpec((tm, LANES), lambda bi, i: (i, 0)),
    ]
    out_shape = [
        jax.ShapeDtypeStruct((b, MLA_HEADS, ns, MLA_HEAD_PAD, ATT_TILE), BF16),
        jax.ShapeDtypeStruct((b, MLA_HEADS, s, MLA_HEAD_PAD), BF16),
        jax.ShapeDtypeStruct((b, MLA_HEADS, ns, MLA_V, ATT_TILE), BF16),
        jax.ShapeDtypeStruct((b, s, MLA_WIDTH), BF16),
        jax.ShapeDtypeStruct((b, s, SIDE_COLS), BF16),
    ]
    out_specs = [
        pl.BlockSpec((1, MLA_HEADS, per, MLA_HEAD_PAD, ATT_TILE), lambda bi, i: (bi, 0, i, 0, 0)),
        pl.BlockSpec((1, MLA_HEADS, tm, MLA_HEAD_PAD), lambda bi, i: (bi, 0, i, 0)),
        pl.BlockSpec((1, MLA_HEADS, per, MLA_V, ATT_TILE), lambda bi, i: (bi, 0, i, 0, 0)),
        pl.BlockSpec((1, tm, MLA_WIDTH), lambda bi, i: (bi, i, 0)),
        pl.BlockSpec((1, tm, SIDE_COLS), lambda bi, i: (bi, i, 0)),
    ]
    return pl.pallas_call(
        _proj_kernel, grid=(b, s // tm), in_specs=in_specs, out_specs=out_specs, out_shape=out_shape,
        compiler_params=pltpu.CompilerParams(dimension_semantics=("arbitrary", "arbitrary"),
                                             vmem_limit_bytes=VMEM_LIMIT),
        name="proj",
    )(x, an, win, qn, wuqt, kvn, wuk, wuvt, cost, sint, cc, sa, sb)


def _memkv_kernel(mem_ref, mn_ref, w_ref, o_ref):
    mn = _rms(mem_ref[0], mn_ref[...]).astype(BF16)
    o_ref[0] = jnp.dot(mn, w_ref[...], preferred_element_type=F32).astype(BF16)


def _memkv_call(mem, mn, w):
    b, m, _ = mem.shape
    return pl.pallas_call(
        _memkv_kernel, grid=(b,),
        in_specs=[pl.BlockSpec((1, m, D_MODEL), lambda bi: (bi, 0, 0)), _const_spec(mn.shape), _const_spec(w.shape)],
        out_specs=pl.BlockSpec((1, m, 2 * MEM_WIDTH), lambda bi: (bi, 0, 0)),
        out_shape=jax.ShapeDtypeStruct((b, m, 2 * MEM_WIDTH), BF16),
        compiler_params=pltpu.CompilerParams(dimension_semantics=("arbitrary",), vmem_limit_bytes=VMEM_LIMIT),
        name="memkv",
    )(mem, mn, w)


HEADS_PER_STEP = 2
MAX_SLABS = 4
TAIL_BLOCKS = 4
MLA_UNROLL = 8


def _mla_kernel(qt_ref, k_ref, vt_ref, g_ref, o_ref, s0, s1, p0, p1, m_scr, l_scr, acc_scr, acc2_scr, *, ns):
    t = ATT_TILE
    heads = range(HEADS_PER_STEP)
    unroll = MLA_UNROLL
    s_scr = (s0, s1)
    p_scr = (p0, p1)
    key_idx = lax.broadcasted_iota(jnp.int32, (t, t), 0)
    qry_idx = lax.broadcasted_iota(jnp.int32, (t, t), 1)
    causal = key_idx <= qry_idx

    def scores(par, tile, kj, masked):
        for hh in heads:
            kb = k_ref[0, hh, pl.ds(pl.multiple_of(kj * t, t), t), :]
            st = jnp.dot(kb, qt_ref[0, hh, tile], preferred_element_type=F32)
            if masked:
                st = jnp.where(causal, st, NEG)
            s_scr[par][hh, kj] = st
            slab = jnp.max(st.reshape(MAX_SLABS, t // MAX_SLABS, t), axis=0)
            m_scr[par, hh] = jnp.maximum(m_scr[par, hh], jnp.max(slab, axis=0, keepdims=True))

    def exps(par, m_fin, kj):
        for hh in heads:
            p = jnp.exp2(s_scr[par][hh, kj] - m_fin[hh])
            l_scr[par, hh] += jnp.sum(p, axis=0, keepdims=True)
            p_scr[par][hh, kj] = p.astype(BF16)

    def pv(par, kj, acc=acc_scr):
        for hh in heads:
            acc[hh] += jnp.dot(vt_ref[0, hh, kj], p_scr[par][hh, kj], preferred_element_type=F32)

    def start_scores(par):
        for hh in heads:
            m_scr[par, hh] = jnp.full((1, t), NEG, F32)

    def start_exps(par):
        m_fin = [m_scr[par, hh] for hh in heads]
        for hh in heads:
            l_scr[par, hh] = jnp.zeros((1, t), F32)
        return m_fin

    def start_pv(acc=acc_scr):
        for hh in heads:
            acc[hh] = jnp.zeros((MLA_V, t), F32)

    def finish(par, tile, acc=acc_scr):
        o_t = jnp.concatenate([acc[hh] / l_scr[par, hh] for hh in heads], axis=0)
        r0 = pl.multiple_of(tile * t, t)
        gate = g_ref[0, pl.ds(r0, t), :].astype(F32)
        o_ref[0, pl.ds(r0, t), :] = (o_t.T * gate).astype(BF16)

    def run_blocks(n, fns, tail):
        def span(base, width):
            for fn in fns:
                for u in range(width):
                    fn(base + u)

        def body(i, c):
            span(i * unroll, unroll)
            return c

        lax.fori_loop(0, n // unroll, body, 0)
        base = (n // unroll) * unroll
        width = unroll // 2
        while width >= TAIL_BLOCKS:
            take = (n - base) & width
            if isinstance(take, int):
                if take:
                    span(base, width)
            else:
                pl.when(take != 0)(functools.partial(span, base, width))
            base = base + take
            width //= 2
        rem = n - base

        def rest(r):
            span(base, r)
            tail()

        if isinstance(rem, int):
            rest(rem)
        else:
            for r in range(TAIL_BLOCKS):
                pl.when(rem == r)(functools.partial(rest, r))

    def step(par, qi):
        m_fin = start_exps(1 - par)
        start_scores(par)
        start_pv()

        def tail():
            exps(1 - par, m_fin, qi - 1)
            scores(par, qi, qi - 1, False)
            scores(par, qi, qi, True)
            finish(par, qi - 2)

        run_blocks(qi - 1, [functools.partial(pv, par), functools.partial(exps, 1 - par, m_fin),
                            lambda kj: scores(par, qi, kj, False)], tail)

    def step_pair(i, carry):
        step(0, 2 * i)
        step(1, 2 * i + 1)
        return carry

    start_scores(0)
    scores(0, 0, 0, True)
    m_fin = start_exps(0)
    start_scores(1)
    exps(0, m_fin, 0)
    scores(1, 1, 0, False)
    scores(1, 1, 1, True)
    lax.fori_loop(1, ns // 2, step_pair, 0)
    m_fin = start_exps(1)
    start_pv()
    start_pv(acc2_scr)
    for kj in range(ns):
        if kj < ns - 1:
            pv(0, kj)
        exps(1, m_fin, kj)
    for kj in range(ns):
        pv(1, kj, acc2_scr)
    finish(0, ns - 2)
    finish(1, ns - 1, acc2_scr)


def _mla_call(qt, k, vt, g):
    b, _, ns, _, t = qt.shape
    s = ns * t
    hp = HEADS_PER_STEP
    return pl.pallas_call(
        functools.partial(_mla_kernel, ns=ns), grid=(b, MLA_HEADS // hp),
        in_specs=[
            pl.BlockSpec((1, hp, ns, MLA_HEAD_PAD, t), lambda bi, h: (bi, h, 0, 0, 0)),
            pl.BlockSpec((1, hp, s, MLA_HEAD_PAD), lambda bi, h: (bi, h, 0, 0)),
            pl.BlockSpec((1, hp, ns, MLA_V, t), lambda bi, h: (bi, h, 0, 0, 0)),
            pl.BlockSpec((1, s, hp * MLA_V), lambda bi, h: (bi, 0, h)),
        ],
        out_specs=pl.BlockSpec((1, s, hp * MLA_V), lambda bi, h: (bi, 0, h)),
        out_shape=jax.ShapeDtypeStruct((b, s, MLA_WIDTH), BF16),
        scratch_shapes=[
            pltpu.VMEM((hp, ns, t, t), F32),
            pltpu.VMEM((hp, ns, t, t), F32),
            pltpu.VMEM((hp, ns, t, t), BF16),
            pltpu.VMEM((hp, ns, t, t), BF16),
            pltpu.VMEM((2, hp, 1, t), F32),
            pltpu.VMEM((2, hp, 1, t), F32),
            pltpu.VMEM((hp, MLA_V, t), F32),
            pltpu.VMEM((hp, MLA_V, t), F32),
        ],
        compiler_params=pltpu.CompilerParams(dimension_semantics=("arbitrary", "arbitrary"),
                                             vmem_limit_bytes=VMEM_LIMIT),
        name="mla_attn",
    )(qt, k, vt, g)


def _side_kernel(sinks_ref, side_ref, prev_ref, kvm_ref, x_ref, ya_ref, w_ref, fn_ref, o_ref, bias_scr, y_scr, *,
                 final):
    w = WINDOW
    tq = side_ref.shape[1]
    kj = lax.broadcasted_iota(jnp.int32, (w, 2 * w), 1)

    @pl.when((pl.program_id(0) == 0) & (pl.program_id(1) == 0))
    def _():
        qi = lax.broadcasted_iota(jnp.int32, (w, 2 * w), 0)
        dist = w + qi - kj
        in_window = (dist >= 0) & (dist < w)
        distf = dist.astype(F32)
        for head in range(SWA_HEADS):
            slope = 2.0 ** (-8.0 * (head + 1) / SWA_HEADS) * LOG2_E
            bias_scr[head] = jnp.where(in_window, -slope * distf, NEG)

    prev_ok = (kj >= w) | (pl.program_id(1) > 0)
    lane = lax.broadcasted_iota(jnp.int32, (w, LANES), 1)
    lo = lane < SWA_HEAD_DIM
    lane_kv = lax.broadcasted_iota(jnp.int32, (2 * w, LANES), 1)
    lo_kv = lane_kv < SWA_HEAD_DIM

    for n in range(tq // w):
        rows = slice(n * w, (n + 1) * w)
        kcol = slice(S_KSWA, S_KSWA + LANES)
        vcol = slice(S_VSWA, S_VSWA + LANES)
        if n == 0:
            k_prev = prev_ref[0, :, 0:LANES]
            v_prev = prev_ref[0, :, LANES:2 * LANES]
        else:
            k_prev = side_ref[0, (n - 1) * w:n * w, kcol]
            v_prev = side_ref[0, (n - 1) * w:n * w, vcol]
        k_both = jnp.concatenate([k_prev, side_ref[0, rows, kcol]], axis=0)
        v_both = jnp.concatenate([v_prev, side_ref[0, rows, vcol]], axis=0)
        k_swap = pltpu.roll(k_both, SWA_HEAD_DIM, 1)
        v_swap = pltpu.roll(v_both, SWA_HEAD_DIM, 1)
        zero_kv = jnp.zeros_like(v_both)
        for kv in range(SWA_KV_HEADS):
            own_lo, own_hi = (k_both, k_swap) if kv == 0 else (k_swap, k_both)
            kcat = jnp.where(lo_kv, own_lo, own_hi)
            v_lo = jnp.where(lo_kv, v_both if kv == 0 else v_swap, zero_kv)
            v_hi = jnp.where(lo_kv, zero_kv, v_swap if kv == 0 else v_both)
            q_parts = []
            for j in range(SWA_GROUP // 2):
                qcol = slice(S_QSWA + (kv * SWA_GROUP + 2 * j) * SWA_HEAD_DIM,
                             S_QSWA + (kv * SWA_GROUP + 2 * j + 2) * SWA_HEAD_DIM)
                qp = side_ref[0, rows, qcol]
                q_parts.append(jnp.where(lo, qp, jnp.zeros_like(qp)))
                q_parts.append(jnp.where(lo, jnp.zeros_like(qp), qp))
            qs = jnp.concatenate(q_parts, axis=0)
            sc = lax.dot_general(qs, kcat, NT_DIMS, preferred_element_type=F32)
            probs = []
            for g in range(SWA_GROUP):
                head = kv * SWA_GROUP + g
                sink = sinks_ref[head] * LOG2_E
                sg = sc[g * w:(g + 1) * w] + bias_scr[head]
                if n == 0:
                    sg = jnp.where(prev_ok, sg, NEG)
                m = jnp.maximum(jnp.max(sg, axis=-1, keepdims=True), sink)
                e = jnp.exp2(sg - m)
                den = jnp.sum(e, axis=-1, keepdims=True) + jnp.exp2(sink - m)
                probs.append((e * (1.0 / den)).astype(BF16))
            for j in range(SWA_GROUP // 2):
                out = (jnp.dot(probs[2 * j], v_lo, preferred_element_type=F32)
                       + jnp.dot(probs[2 * j + 1], v_hi, preferred_element_type=F32))
                c0 = (kv * SWA_GROUP + 2 * j) * SWA_HEAD_DIM
                gate = side_ref[0, rows, S_GSWA + c0:S_GSWA + c0 + LANES].astype(F32)
                y_scr[rows, c0:c0 + LANES] = (out * gate).astype(BF16)

    for h in range(MEM_HEADS):
        hc = slice(h * MEM_HEAD_DIM, (h + 1) * MEM_HEAD_DIM)
        q = side_ref[0, :, S_QMEM + h * MEM_HEAD_DIM:S_QMEM + (h + 1) * MEM_HEAD_DIM]
        sc = lax.dot_general(q, kvm_ref[0, :, hc], NT_DIMS, preferred_element_type=F32)
        m = jnp.max(sc, axis=-1, keepdims=True)
        e = jnp.exp2(sc - m)
        p = (e * (1.0 / jnp.sum(e, axis=-1, keepdims=True))).astype(BF16)
        out = jnp.dot(p, kvm_ref[0, :, MEM_WIDTH + h * MEM_HEAD_DIM:MEM_WIDTH + (h + 1) * MEM_HEAD_DIM],
                      preferred_element_type=F32)
        gate = side_ref[0, :, S_GMEM + h * MEM_HEAD_DIM:S_GMEM + (h + 1) * MEM_HEAD_DIM].astype(F32)
        y_scr[:, SWA_WIDTH + h * MEM_HEAD_DIM:SWA_WIDTH + (h + 1) * MEM_HEAD_DIM] = (out * gate).astype(BF16)

    y = (x_ref[0]
         + jnp.dot(ya_ref[0], w_ref[0:MLA_WIDTH, :], preferred_element_type=F32)
         + jnp.dot(y_scr[...], w_ref[MLA_WIDTH:, :], preferred_element_type=F32))
    if final:
        y = _rms(y, fn_ref[...])
    o_ref[0] = y


def _side_out_call(sinks, side, kvm, x, y_mla, w_out, fn, final):
    b, s, _ = side.shape
    tq = SIDE_TILE
    per = tq // WINDOW
    m = kvm.shape[1]
    kv_cols = 2 * SWA_KV_WIDTH
    return pl.pallas_call(
        functools.partial(_side_kernel, final=final), grid=(b, s // tq),
        in_specs=[
            pl.BlockSpec(memory_space=pltpu.SMEM),
            pl.BlockSpec((1, tq, SIDE_COLS), lambda bi, i: (bi, i, 0)),
            pl.BlockSpec((1, WINDOW, kv_cols), lambda bi, i: (bi, jnp.maximum(i * per - 1, 0), S_KSWA // kv_cols)),
            pl.BlockSpec((1, m, 2 * MEM_WIDTH), lambda bi, i: (bi, 0, 0)),
            pl.BlockSpec((1, tq, D_MODEL), lambda bi, i: (bi, i, 0)),
            pl.BlockSpec((1, tq, MLA_WIDTH), lambda bi, i: (bi, i, 0)),
            _const_spec(w_out.shape), _const_spec(fn.shape),
        ],
        out_specs=pl.BlockSpec((1, tq, D_MODEL), lambda bi, i: (bi, i, 0)),
        out_shape=jax.ShapeDtypeStruct((b, s, D_MODEL), F32),
        scratch_shapes=[
            pltpu.VMEM((SWA_HEADS, WINDOW, 2 * WINDOW), F32),
            pltpu.VMEM((tq, SWA_WIDTH + MEM_WIDTH), BF16),
        ],
        compiler_params=pltpu.CompilerParams(dimension_semantics=("arbitrary", "arbitrary"),
                                             vmem_limit_bytes=VMEM_LIMIT),
        name="side_out",
    )(sinks, side, side, kvm, x, y_mla, w_out, fn)


def _pack_w_in(w):
    c_q, c_kv, k_rope, z_mla, q_swa, k_swa, v_swa, z_swa, q_mem, z_mem = jnp.split(w, SPLIT_IDX, axis=1)
    kr = jnp.pad(k_rope, ((0, 0), (MLA_NOPE, LANES - MLA_NOPE - MLA_ROPE)))
    return jnp.concatenate([c_q, c_kv, kr, z_mla, q_swa, k_swa, v_swa, z_swa, q_mem, z_mem],
                           axis=1).astype(BF16)


def _pack_w_uq_t(w):
    return w.T.astype(BF16)


def _pack_w_ukv(w):
    w3 = w.reshape(MLA_KV_RANK, MLA_HEADS, MLA_NOPE + MLA_V)
    wk = w3[:, :, :MLA_NOPE].reshape(MLA_KV_RANK, MLA_HEADS * MLA_NOPE).astype(BF16)
    wvt = w3[:, :, MLA_NOPE:].reshape(MLA_KV_RANK, MLA_WIDTH).T.astype(BF16)
    return wk, wvt


def _rope_tables(s):
    inv = ROPE_BASE ** (-jnp.arange(0, MLA_ROPE, 2, dtype=F32) / MLA_ROPE)
    ang = jnp.arange(s, dtype=F32)[:, None] * inv[None, :]
    cos, sin = jnp.cos(ang), jnp.sin(ang)
    lead = jnp.zeros((s, MLA_NOPE), F32)
    tail = jnp.zeros((s, LANES - MLA_NOPE - MLA_ROPE), F32)
    half = jnp.zeros((s, HALF_ROPE), F32)
    cc = jnp.concatenate([lead, cos, cos, tail], axis=1)
    sa = jnp.concatenate([lead, -sin, half, tail], axis=1)
    sb = jnp.concatenate([lead, half, sin, tail], axis=1)
    return cos.T, sin.T, cc, sa, sb


def kernel(x, mem, attn_norm, w_in, mla_q_norm, w_uq, mla_kv_norm, w_ukv, swa_sinks, mem_norm, w_mem_kv, w_out,
           final_norm):
    depth = w_in.shape[0]
    s = x.shape[1]
    assert s % PROJ_TILE == 0 and PROJ_TILE % ATT_TILE == 0 and s % SIDE_TILE == 0 and s % (2 * ATT_TILE) == 0
    cost, sint, cc, sa, sb = _rope_tables(s)
    fn = final_norm.reshape(1, D_MODEL)
    for l in range(depth):
        wk, wvt = _pack_w_ukv(w_ukv[l])
        qt, k, vt, gmla, side = _proj_call(
            x, attn_norm[l].reshape(1, D_MODEL), _pack_w_in(w_in[l]), mla_q_norm[l].reshape(1, MLA_Q_RANK),
            _pack_w_uq_t(w_uq[l]), mla_kv_norm[l].reshape(1, MLA_KV_RANK), wk, wvt, cost, sint, cc, sa, sb)
        kvm = _memkv_call(mem, mem_norm[l].reshape(1, D_MODEL), w_mem_kv[l].astype(BF16))
        y_mla = _mla_call(qt, k, vt, gmla)
        x = _side_out_call(swa_sinks[l], side, kvm, x, y_mla, w_out[l].astype(BF16), fn, final=(l == depth - 1))
    return x
```

```python
import functools

import jax
import jax.numpy as jnp
import numpy as np
from jax import lax
from jax.experimental import pallas as pl
from jax.experimental.pallas import tpu as pltpu

F32 = jnp.float32
BF16 = jnp.bfloat16

D_MODEL = 1024
MLA_HEADS = 16
MLA_NOPE = 64
MLA_ROPE = 32
MLA_V = 64
MLA_Q_RANK = 384
MLA_KV_RANK = 256
ROPE_BASE = 10000.0
SWA_HEADS = 8
SWA_KV_HEADS = 2
SWA_HEAD_DIM = 64
SWA_GROUP = SWA_HEADS // SWA_KV_HEADS
WINDOW = 128
MEM_HEADS = 4
MEM_HEAD_DIM = 128
EPS = 1e-6
NEG = -1e30
LOG2_E = 1.4426950408889634

MLA_WIDTH = MLA_HEADS * MLA_V
SWA_WIDTH = SWA_HEADS * SWA_HEAD_DIM
SWA_KV_WIDTH = SWA_KV_HEADS * SWA_HEAD_DIM
MEM_WIDTH = MEM_HEADS * MEM_HEAD_DIM
SPLITS = (MLA_Q_RANK, MLA_KV_RANK, MLA_ROPE, MLA_WIDTH, SWA_WIDTH, SWA_KV_WIDTH, SWA_KV_WIDTH,
          SWA_WIDTH, MEM_WIDTH, MEM_WIDTH)
SPLIT_IDX = tuple(int(i) for i in np.cumsum(SPLITS)[:-1])

LANES = 128
MLA_HEAD_PAD = 128
HALF_ROPE = MLA_ROPE // 2
ATT_TILE = 256
PROJ_TILE = 512
SIDE_TILE = 512
VMEM_LIMIT = 56 * 1024 * 1024

C_Q = 0
C_KV = C_Q + MLA_Q_RANK
C_KR = C_KV + MLA_KV_RANK
C_ZMLA = C_KR + LANES
C_QSWA = C_ZMLA + MLA_WIDTH
C_KSWA = C_QSWA + SWA_WIDTH
C_VSWA = C_KSWA + SWA_KV_WIDTH
C_ZSWA = C_VSWA + SWA_KV_WIDTH
C_QMEM = C_ZSWA + SWA_WIDTH
C_ZMEM = C_QMEM + MEM_WIDTH
W_IN_COLS = C_ZMEM + MEM_WIDTH

S_QSWA = 0
S_KSWA = S_QSWA + SWA_WIDTH
S_VSWA = S_KSWA + SWA_KV_WIDTH
S_GSWA = S_VSWA + SWA_KV_WIDTH
S_QMEM = S_GSWA + SWA_WIDTH
S_GMEM = S_QMEM + MEM_WIDTH
SIDE_COLS = S_GMEM + MEM_WIDTH

NT_DIMS = (((1,), (1,)), ((), ()))


def _rms(x, g):
    return x * lax.rsqrt(jnp.mean(x * x, axis=-1, keepdims=True) + EPS) * g


def _silu(z):
    return z * jax.nn.sigmoid(z)


def _const_spec(shape):
    nd = len(shape)
    return pl.BlockSpec(shape, lambda *_: (0,) * nd, pipeline_mode=pl.Buffered(1))


def _proj_kernel(x_ref, an_ref, win_ref, qn_ref, wuqt_ref, kvn_ref, wuk_ref, wuvt_ref,
                 cost_ref, sint_ref, cc_ref, sa_ref, sb_ref,
                 qt_ref, k_ref, vt_ref, gmla_ref, side_ref):
    hn = _rms(x_ref[0], an_ref[...]).astype(BF16)

    def proj(a, b):
        return jnp.dot(hn, win_ref[:, a:b], preferred_element_type=F32)

    cqn = _rms(proj(C_Q, C_KV), qn_ref[...]).astype(BF16)
    scale = (MLA_NOPE + MLA_ROPE) ** -0.5 * LOG2_E
    qt = lax.dot_general(wuqt_ref[...], cqn, NT_DIMS, preferred_element_type=F32) * scale
    c = cost_ref[...]
    s = sint_ref[...]
    zero_rows = jnp.zeros((MLA_HEAD_PAD - MLA_NOPE - MLA_ROPE, ATT_TILE), BF16)
    att_tiles = [slice(a * ATT_TILE, (a + 1) * ATT_TILE) for a in range(qt.shape[1] // ATT_TILE)]
    for h in range(MLA_HEADS):
        r = h * (MLA_NOPE + MLA_ROPE)
        x1 = qt[r + MLA_NOPE:r + MLA_NOPE + HALF_ROPE]
        x2 = qt[r + MLA_NOPE + HALF_ROPE:r + MLA_NOPE + MLA_ROPE]
        nope = qt[r:r + MLA_NOPE].astype(BF16)
        r1 = (x1 * c - x2 * s).astype(BF16)
        r2 = (x1 * s + x2 * c).astype(BF16)
        for a, cols in enumerate(att_tiles):
            qt_ref[0, h, a, 0:MLA_NOPE, :] = nope[:, cols]
            qt_ref[0, h, a, MLA_NOPE:MLA_NOPE + HALF_ROPE, :] = r1[:, cols]
            qt_ref[0, h, a, MLA_NOPE + HALF_ROPE:MLA_NOPE + MLA_ROPE, :] = r2[:, cols]
            qt_ref[0, h, a, MLA_NOPE + MLA_ROPE:MLA_HEAD_PAD, :] = zero_rows

    ckvn = _rms(proj(C_KV, C_KR), kvn_ref[...]).astype(BF16)
    kr = proj(C_KR, C_ZMLA)
    kpe = (kr * cc_ref[...] + pltpu.roll(kr, LANES - HALF_ROPE, 1) * sa_ref[...]
           + pltpu.roll(kr, HALF_ROPE, 1) * sb_ref[...])
    kk = jnp.dot(ckvn, wuk_ref[...], preferred_element_type=F32)
    lo = lax.broadcasted_iota(jnp.int32, kpe.shape, 1) < MLA_NOPE
    for h2 in range(MLA_HEADS // 2):
        pair = kk[:, h2 * LANES:(h2 + 1) * LANES]
        k_ref[0, 2 * h2, :, :] = jnp.where(lo, pair, kpe).astype(BF16)
        k_ref[0, 2 * h2 + 1, :, :] = jnp.where(lo, pltpu.roll(pair, MLA_NOPE, 1), kpe).astype(BF16)

    vt = lax.dot_general(wuvt_ref[...], ckvn, NT_DIMS, preferred_element_type=F32)
    for h in range(MLA_HEADS):
        for a, cols in enumerate(att_tiles):
            vt_ref[0, h, a, :, :] = vt[h * MLA_V:(h + 1) * MLA_V, cols].astype(BF16)

    gmla_ref[0] = _silu(proj(C_ZMLA, C_QSWA)).astype(BF16)
    side_ref[0, :, S_QSWA:S_KSWA] = (proj(C_QSWA, C_KSWA) * (SWA_HEAD_DIM ** -0.5 * LOG2_E)).astype(BF16)
    side_ref[0, :, S_KSWA:S_GSWA] = proj(C_KSWA, C_ZSWA).astype(BF16)
    side_ref[0, :, S_GSWA:S_QMEM] = _silu(proj(C_ZSWA, C_QMEM)).astype(BF16)
    side_ref[0, :, S_QMEM:S_GMEM] = (proj(C_QMEM, C_ZMEM) * (MEM_HEAD_DIM ** -0.5 * LOG2_E)).astype(BF16)
    side_ref[0, :, S_GMEM:SIDE_COLS] = _silu(proj(C_ZMEM, W_IN_COLS)).astype(BF16)


def _proj_call(x, an, win, qn, wuqt, kvn, wuk, wuvt, cost, sint, cc, sa, sb):
    b, s, _ = x.shape
    tm = PROJ_TILE
    per = tm // ATT_TILE
    ns = s // ATT_TILE
    in_specs = [
        pl.BlockSpec((1, tm, D_MODEL), lambda bi, i: (bi, i, 0)),
        _const_spec(an.shape), _const_spec(win.shape), _const_spec(qn.shape), _const_spec(wuqt.shape),
        _const_spec(kvn.shape), _const_spec(wuk.shape), _const_spec(wuvt.shape),
        pl.BlockSpec((HALF_ROPE, tm), lambda bi, i: (0, i)),
        pl.BlockSpec((HALF_ROPE, tm), lambda bi, i: (0, i)),
        pl.BlockSpec((tm, LANES), lambda bi, i: (i, 0)),
        pl.BlockSpec((tm, LANES), lambda bi, i: (i, 0)),
        pl.BlockSpec((tm, LANES), lambda bi, i: (i, 0)),
    ]
    out_shape = [
        jax.ShapeDtypeStruct((b, MLA_HEADS, ns, MLA_HEAD_PAD, ATT_TILE), BF16),
        jax.ShapeDtypeStruct((b, MLA_HEADS, s, MLA_HEAD_PAD), BF16),
        jax.ShapeDtypeStruct((b, MLA_HEADS, ns, MLA_V, ATT_TILE), BF16),
        jax.ShapeDtypeStruct((b, s, MLA_WIDTH), BF16),
        jax.ShapeDtypeStruct((b, s, SIDE_COLS), BF16),
    ]
    out_specs = [
        pl.BlockSpec((1, MLA_HEADS, per, MLA_HEAD_PAD, ATT_TILE), lambda bi, i: (bi, 0, i, 0, 0)),
        pl.BlockSpec((1, MLA_HEADS, tm, MLA_HEAD_PAD), lambda bi, i: (bi, 0, i, 0)),
        pl.BlockSpec((1, MLA_HEADS, per, MLA_V, ATT_TILE), lambda bi, i: (bi, 0, i, 0, 0)),
        pl.BlockSpec((1, tm, MLA_WIDTH), lambda bi, i: (bi, i, 0)),
        pl.BlockSpec((1, tm, SIDE_COLS), lambda bi, i: (bi, i, 0)),
    ]
    return pl.pallas_call(
        _proj_kernel, grid=(b, s // tm), in_specs=in_specs, out_specs=out_specs, out_shape=out_shape,
        compiler_params=pltpu.CompilerParams(dimension_semantics=("arbitrary", "arbitrary"),
                                             vmem_limit_bytes=VMEM_LIMIT),
        name="proj",
    )(x, an, win, qn, wuqt, kvn, wuk, wuvt, cost, sint, cc, sa, sb)


def _memkv_kernel(mem_ref, mn_ref, w_ref, o_ref):
    mn = _rms(mem_ref[0], mn_ref[...]).astype(BF16)
    o_ref[0] = jnp.dot(mn, w_ref[...], preferred_element_type=F32).astype(BF16)


def _memkv_call(mem, mn, w):
    b, m, _ = mem.shape
    return pl.pallas_call(
        _memkv_kernel, grid=(b,),
        in_specs=[pl.BlockSpec((1, m, D_MODEL), lambda bi: (bi, 0, 0)), _const_spec(mn.shape), _const_spec(w.shape)],
        out_specs=pl.BlockSpec((1, m, 2 * MEM_WIDTH), lambda bi: (bi, 0, 0)),
        out_shape=jax.ShapeDtypeStruct((b, m, 2 * MEM_WIDTH), BF16),
        compiler_params=pltpu.CompilerParams(dimension_semantics=("arbitrary",), vmem_limit_bytes=VMEM_LIMIT),
        name="memkv",
    )(mem, mn, w)


HEADS_PER_STEP = 2
MAX_SLABS = 4
TAIL_BLOCKS = 4
MLA_UNROLL = 8


def _mla_kernel(qt_ref, k_ref, vt_ref, g_ref, o_ref, s0, s1, p0, p1, m_scr, l_scr, acc_scr, acc2_scr, *, ns):
    t = ATT_TILE
    heads = range(HEADS_PER_STEP)
    unroll = MLA_UNROLL
    s_scr = (s0, s1)
    p_scr = (p0, p1)
    key_idx = lax.broadcasted_iota(jnp.int32, (t, t), 0)
    qry_idx = lax.broadcasted_iota(jnp.int32, (t, t), 1)
    causal = key_idx <= qry_idx

    def scores(par, tile, kj, masked):
        for hh in heads:
            kb = k_ref[0, hh, pl.ds(pl.multiple_of(kj * t, t), t), :]
            st = jnp.dot(kb, qt_ref[0, hh, tile], preferred_element_type=F32)
            if masked:
                st = jnp.where(causal, st, NEG)
            s_scr[par][hh, kj] = st
            slab = jnp.max(st.reshape(MAX_SLABS, t // MAX_SLABS, t), axis=0)
            m_scr[par, hh] = jnp.maximum(m_scr[par, hh], jnp.max(slab, axis=0, keepdims=True))

    def exps(par, m_fin, kj):
        for hh in heads:
            p = jnp.exp2(s_scr[par][hh, kj] - m_fin[hh])
            l_scr[par, hh] += jnp.sum(p, axis=0, keepdims=True)
            p_scr[par][hh, kj] = p.astype(BF16)

    def pv(par, kj, acc=acc_scr):
        for hh in heads:
            acc[hh] += jnp.dot(vt_ref[0, hh, kj], p_scr[par][hh, kj], preferred_element_type=F32)

    def start_scores(par):
        for hh in heads:
            m_scr[par, hh] = jnp.full((1, t), NEG, F32)

    def start_exps(par):
        m_fin = [m_scr[par, hh] for hh in heads]
        for hh in heads:
            l_scr[par, hh] = jnp.zeros((1, t), F32)
        return m_fin

    def start_pv(acc=acc_scr):
        for hh in heads:
            acc[hh] = jnp.zeros((MLA_V, t), F32)

    def finish(par, tile, acc=acc_scr):
        o_t = jnp.concatenate([acc[hh] / l_scr[par, hh] for hh in heads], axis=0)
        r0 = pl.multiple_of(tile * t, t)
        gate = g_ref[0, pl.ds(r0, t), :].astype(F32)
        o_ref[0, pl.ds(r0, t), :] = (o_t.T * gate).astype(BF16)

    def run_blocks(n, fns, tail):
        def span(base, width):
            for fn in fns:
                for u in range(width):
                    fn(base + u)

        def body(i, c):
            span(i * unroll, unroll)
            return c

        lax.fori_loop(0, n // unroll, body, 0)
        base = (n // unroll) * unroll
        width = unroll // 2
        while width >= TAIL_BLOCKS:
            take = (n - base) & width
            if isinstance(take, int):
                if take:
                    span(base, width)
            else:
                pl.when(take != 0)(functools.partial(span, base, width))
            base = base + take
            width //= 2
        rem = n - base

        def rest(r):
            span(base, r)
            tail()

        if isinstance(rem, int):
            rest(rem)
        else:
            for r in range(TAIL_BLOCKS):
                pl.when(rem == r)(functools.partial(rest, r))

    def step(par, qi):
        m_fin = start_exps(1 - par)
        start_scores(par)
        start_pv()

        def tail():
            exps(1 - par, m_fin, qi - 1)
            scores(par, qi, qi - 1, False)
            scores(par, qi, qi, True)
            finish(par, qi - 2)

        run_blocks(qi - 1, [functools.partial(pv, par), functools.partial(exps, 1 - par, m_fin),
                            lambda kj: scores(par, qi, kj, False)], tail)

    def step_pair(i, carry):
        step(0, 2 * i)
        step(1, 2 * i + 1)
        return carry

    start_scores(0)
    scores(0, 0, 0, True)
    m_fin = start_exps(0)
    start_scores(1)
    exps(0, m_fin, 0)
    scores(1, 1, 0, False)
    scores(1, 1, 1, True)
    lax.fori_loop(1, ns // 2, step_pair, 0)
    m_fin = start_exps(1)
    start_pv()
    start_pv(acc2_scr)
    for kj in range(ns):
        if kj < ns - 1:
            pv(0, kj)
        exps(1, m_fin, kj)
    for kj in range(ns):
        pv(1, kj, acc2_scr)
    finish(0, ns - 2)
    finish(1, ns - 1, acc2_scr)


def _mla_call(qt, k, vt, g):
    b, _, ns, _, t = qt.shape
    s = ns * t
    hp = HEADS_PER_STEP
    return pl.pallas_call(
        functools.partial(_mla_kernel, ns=ns), grid=(b, MLA_HEADS // hp),
        in_specs=[
            pl.BlockSpec((1, hp, ns, MLA_HEAD_PAD, t), lambda bi, h: (bi, h, 0, 0, 0)),
            pl.BlockSpec((1, hp, s, MLA_HEAD_PAD), lambda bi, h: (bi, h, 0, 0)),
            pl.BlockSpec((1, hp, ns, MLA_V, t), lambda bi, h: (bi, h, 0, 0, 0)),
            pl.BlockSpec((1, s, hp * MLA_V), lambda bi, h: (bi, 0, h)),
        ],
        out_specs=pl.BlockSpec((1, s, hp * MLA_V), lambda bi, h: (bi, 0, h)),
        out_shape=jax.ShapeDtypeStruct((b, s, MLA_WIDTH), BF16),
        scratch_shapes=[
            pltpu.VMEM((hp, ns, t, t), F32),
            pltpu.VMEM((hp, ns, t, t), F32),
            pltpu.VMEM((hp, ns, t, t), BF16),
            pltpu.VMEM((hp, ns, t, t), BF16),
            pltpu.VMEM((2, hp, 1, t), F32),
            pltpu.VMEM((2, hp, 1, t), F32),
            pltpu.VMEM((hp, MLA_V, t), F32),
            pltpu.VMEM((hp, MLA_V, t), F32),
        ],
        compiler_params=pltpu.CompilerParams(dimension_semantics=("arbitrary", "arbitrary"),
                                             vmem_limit_bytes=VMEM_LIMIT),
        name="mla_attn",
    )(qt, k, vt, g)


def _side_kernel(sinks_ref, side_ref, prev_ref, kvm_ref, x_ref, ya_ref, w_ref, fn_ref, o_ref, bias_scr, y_scr, *,
                 final):
    w = WINDOW
    tq = side_ref.shape[1]
    kj = lax.broadcasted_iota(jnp.int32, (w, 2 * w), 1)

    @pl.when((pl.program_id(0) == 0) & (pl.program_id(1) == 0))
    def _():
        qi = lax.broadcasted_iota(jnp.int32, (w, 2 * w), 0)
        dist = w + qi - kj
        in_window = (dist >= 0) & (dist < w)
        distf = dist.astype(F32)
        for head in range(SWA_HEADS):
            slope = 2.0 ** (-8.0 * (head + 1) / SWA_HEADS) * LOG2_E
            bias_scr[head] = jnp.where(in_window, -slope * distf, NEG)

    prev_ok = (kj >= w) | (pl.program_id(1) > 0)
    lane = lax.broadcasted_iota(jnp.int32, (w, LANES), 1)
    lo = lane < SWA_HEAD_DIM
    lane_kv = lax.broadcasted_iota(jnp.int32, (2 * w, LANES), 1)
    lo_kv = lane_kv < SWA_HEAD_DIM

    for n in range(tq // w):
        rows = slice(n * w, (n + 1) * w)
        kcol = slice(S_KSWA, S_KSWA + LANES)
        vcol = slice(S_VSWA, S_VSWA + LANES)
        if n == 0:
            k_prev = prev_ref[0, :, 0:LANES]
            v_prev = prev_ref[0, :, LANES:2 * LANES]
        else:
            k_prev = side_ref[0, (n - 1) * w:n * w, kcol]
            v_prev = side_ref[0, (n - 1) * w:n * w, vcol]
        k_both = jnp.concatenate([k_prev, side_ref[0, rows, kcol]], axis=0)
        v_both = jnp.concatenate([v_prev, side_ref[0, rows, vcol]], axis=0)
        k_swap = pltpu.roll(k_both, SWA_HEAD_DIM, 1)
        v_swap = pltpu.roll(v_both, SWA_HEAD_DIM, 1)
        zero_kv = jnp.zeros_like(v_both)
        for kv in range(SWA_KV_HEADS):
            own_lo, own_hi = (k_both, k_swap) if kv == 0 else (k_swap, k_both)
            kcat = jnp.where(lo_kv, own_lo, own_hi)
            v_lo = jnp.where(lo_kv, v_both if kv == 0 else v_swap, zero_kv)
            v_hi = jnp.where(lo_kv, zero_kv, v_swap if kv == 0 else v_both)
            q_parts = []
            for j in range(SWA_GROUP // 2):
                qcol = slice(S_QSWA + (kv * SWA_GROUP + 2 * j) * SWA_HEAD_DIM,
                             S_QSWA + (kv * SWA_GROUP + 2 * j + 2) * SWA_HEAD_DIM)
                qp = side_ref[0, rows, qcol]
                q_parts.append(jnp.where(lo, qp, jnp.zeros_like(qp)))
                q_parts.append(jnp.where(lo, jnp.zeros_like(qp), qp))
            qs = jnp.concatenate(q_parts, axis=0)
            sc = lax.dot_general(qs, kcat, NT_DIMS, preferred_element_type=F32)
            ones_kv = jnp.ones_like(v_both)
            v_wide = (jnp.concatenate([v_lo, ones_kv], axis=1), jnp.concatenate([v_hi, ones_kv], axis=1))
            normed = []
            for g in range(SWA_GROUP):
                head = kv * SWA_GROUP + g
                sink = sinks_ref[head] * LOG2_E
                sg = sc[g * w:(g + 1) * w] + bias_scr[head]
                if n == 0:
                    sg = jnp.where(prev_ok, sg, NEG)
                m = jnp.maximum(jnp.max(sg, axis=-1, keepdims=True), sink)
                e = jnp.exp2(sg - m).astype(BF16)
                num_den = jnp.dot(e, v_wide[g % 2], preferred_element_type=F32)
                den = num_den[:, LANES:] + jnp.exp2(sink - m)
                normed.append(num_den[:, :LANES] * (1.0 / den))
            for j in range(SWA_GROUP // 2):
                out = normed[2 * j] + normed[2 * j + 1]
                c0 = (kv * SWA_GROUP + 2 * j) * SWA_HEAD_DIM
                gate = side_ref[0, rows, S_GSWA + c0:S_GSWA + c0 + LANES].astype(F32)
                y_scr[rows, c0:c0 + LANES] = (out * gate).astype(BF16)

    for h in range(MEM_HEADS):
        hc = slice(h * MEM_HEAD_DIM, (h + 1) * MEM_HEAD_DIM)
        q = side_ref[0, :, S_QMEM + h * MEM_HEAD_DIM:S_QMEM + (h + 1) * MEM_HEAD_DIM]
        sc = lax.dot_general(q, kvm_ref[0, :, hc], NT_DIMS, preferred_element_type=F32)
        m = jnp.max(sc, axis=-1, keepdims=True)
        e = jnp.exp2(sc - m).astype(BF16)
        v_ones = jnp.concatenate(
            [kvm_ref[0, :, MEM_WIDTH + h * MEM_HEAD_DIM:MEM_WIDTH + (h + 1) * MEM_HEAD_DIM],
             jnp.ones((kvm_ref.shape[1], MEM_HEAD_DIM), BF16)], axis=1)
        num_den = jnp.dot(e, v_ones, preferred_element_type=F32)
        out = num_den[:, :MEM_HEAD_DIM] * (1.0 / num_den[:, MEM_HEAD_DIM:])
        gate = side_ref[0, :, S_GMEM + h * MEM_HEAD_DIM:S_GMEM + (h + 1) * MEM_HEAD_DIM].astype(F32)
        y_scr[:, SWA_WIDTH + h * MEM_HEAD_DIM:SWA_WIDTH + (h + 1) * MEM_HEAD_DIM] = (out * gate).astype(BF16)

    y = (x_ref[0]
         + jnp.dot(ya_ref[0], w_ref[0:MLA_WIDTH, :], preferred_element_type=F32)
         + jnp.dot(y_scr[...], w_ref[MLA_WIDTH:, :], preferred_element_type=F32))
    if final:
        y = _rms(y, fn_ref[...])
    o_ref[0] = y


def _side_out_call(sinks, side, kvm, x, y_mla, w_out, fn, final):
    b, s, _ = side.shape
    tq = SIDE_TILE
    per = tq // WINDOW
    m = kvm.shape[1]
    kv_cols = 2 * SWA_KV_WIDTH
    return pl.pallas_call(
        functools.partial(_side_kernel, final=final), grid=(b, s // tq),
        in_specs=[
            pl.BlockSpec(memory_space=pltpu.SMEM),
            pl.BlockSpec((1, tq, SIDE_COLS), lambda bi, i: (bi, i, 0)),
            pl.BlockSpec((1, WINDOW, kv_cols), lambda bi, i: (bi, jnp.maximum(i * per - 1, 0), S_KSWA // kv_cols)),
            pl.BlockSpec((1, m, 2 * MEM_WIDTH), lambda bi, i: (bi, 0, 0)),
            pl.BlockSpec((1, tq, D_MODEL), lambda bi, i: (bi, i, 0)),
            pl.BlockSpec((1, tq, MLA_WIDTH), lambda bi, i: (bi, i, 0)),
            _const_spec(w_out.shape), _const_spec(fn.shape),
        ],
        out_specs=pl.BlockSpec((1, tq, D_MODEL), lambda bi, i: (bi, i, 0)),
        out_shape=jax.ShapeDtypeStruct((b, s, D_MODEL), F32),
        scratch_shapes=[
            pltpu.VMEM((SWA_HEADS, WINDOW, 2 * WINDOW), F32),
            pltpu.VMEM((tq, SWA_WIDTH + MEM_WIDTH), BF16),
        ],
        compiler_params=pltpu.CompilerParams(dimension_semantics=("arbitrary", "arbitrary"),
                                             vmem_limit_bytes=VMEM_LIMIT),
        name="side_out",
    )(sinks, side, side, kvm, x, y_mla, w_out, fn)


def _pack_w_in(w):
    c_q, c_kv, k_rope, z_mla, q_swa, k_swa, v_swa, z_swa, q_mem, z_mem = jnp.split(w, SPLIT_IDX, axis=1)
    kr = jnp.pad(k_rope, ((0, 0), (MLA_NOPE, LANES - MLA_NOPE - MLA_ROPE)))
    return jnp.concatenate([c_q, c_kv, kr, z_mla, q_swa, k_swa, v_swa, z_swa, q_mem, z_mem],
                           axis=1).astype(BF16)


def _pack_w_uq_t(w):
    return w.T.astype(BF16)


def _pack_w_ukv(w):
    w3 = w.reshape(MLA_KV_RANK, MLA_HEADS, MLA_NOPE + MLA_V)
    wk = w3[:, :, :MLA_NOPE].reshape(MLA_KV_RANK, MLA_HEADS * MLA_NOPE).astype(BF16)
    wvt = w3[:, :, MLA_NOPE:].reshape(MLA_KV_RANK, MLA_WIDTH).T.astype(BF16)
    return wk, wvt


def _rope_tables(s):
    inv = ROPE_BASE ** (-jnp.arange(0, MLA_ROPE, 2, dtype=F32) / MLA_ROPE)
    ang = jnp.arange(s, dtype=F32)[:, None] * inv[None, :]
    cos, sin = jnp.cos(ang), jnp.sin(ang)
    lead = jnp.zeros((s, MLA_NOPE), F32)
    tail = jnp.zeros((s, LANES - MLA_NOPE - MLA_ROPE), F32)
    half = jnp.zeros((s, HALF_ROPE), F32)
    cc = jnp.concatenate([lead, cos, cos, tail], axis=1)
    sa = jnp.concatenate([lead, -sin, half, tail], axis=1)
    sb = jnp.concatenate([lead, half, sin, tail], axis=1)
    return cos.T, sin.T, cc, sa, sb


def kernel(x, mem, attn_norm, w_in, mla_q_norm, w_uq, mla_kv_norm, w_ukv, swa_sinks, mem_norm, w_mem_kv, w_out,
           final_norm):
    depth = w_in.shape[0]
    s = x.shape[1]
    assert s % PROJ_TILE == 0 and PROJ_TILE % ATT_TILE == 0 and s % SIDE_TILE == 0 and s % (2 * ATT_TILE) == 0
    cost, sint, cc, sa, sb = _rope_tables(s)
    fn = final_norm.reshape(1, D_MODEL)
    for l in range(depth):
        wk, wvt = _pack_w_ukv(w_ukv[l])
        qt, k, vt, gmla, side = _proj_call(
            x, attn_norm[l].reshape(1, D_MODEL), _pack_w_in(w_in[l]), mla_q_norm[l].reshape(1, MLA_Q_RANK),
            _pack_w_uq_t(w_uq[l]), mla_kv_norm[l].reshape(1, MLA_KV_RANK), wk, wvt, cost, sint, cc, sa, sb)
        kvm = _memkv_call(mem, mem_norm[l].reshape(1, D_MODEL), w_mem_kv[l].astype(BF16))
        y_mla = _mla_call(qt, k, vt, gmla)
        x = _side_out_call(swa_sinks[l], side, kvm, x, y_mla, w_out[l].astype(BF16), fn, final=(l == depth - 1))
    return x
```
